```python
import math
import jax
import jax.numpy as jnp
from jax import lax
import numpy as np

D_MODEL = 1024
BATCH = 4
SEQ = 4096
DEPTH = 4
DEC_BATCH = 32
DEC_SEQ = 1
PAST_LEN = 8192
PAGE_SIZE = 128

ATT_HEADS = 8
ATT_HEAD_DIM = 64
ATT_WIDTH = ATT_HEADS * ATT_HEAD_DIM
MOBA_BLOCK = 256
MOBA_TOPK = 3
MOBA_Q_CHUNK = 32
CONV_WIDTH = 512
CONV_K = 3
HG_HEADS = 4
HG_DK = 128
HG_DV = 128
HG_KWIDTH = HG_HEADS * HG_DK
HG_VWIDTH = HG_HEADS * HG_DV
HG_CHUNK = 64
N_BRANCH = 3
D_FF = -(-(8 * D_MODEL) // (3 * 256)) * 256
DEEPNORM_ALPHA = (2 * DEPTH) ** 0.25
DEEPNORM_BETA = (8 * DEPTH) ** -0.25
LN_EPS = 1e-5
NEG_INF = -1e30
IN_SIZES = (ATT_WIDTH, ATT_WIDTH, ATT_WIDTH, CONV_WIDTH, CONV_WIDTH, CONV_WIDTH,
            HG_KWIDTH, HG_KWIDTH, HG_VWIDTH, HG_VWIDTH, N_BRANCH * D_MODEL)
IN_SPLITS = tuple(int(s) for s in np.cumsum(IN_SIZES)[:-1])
N_IN = sum(IN_SIZES)

kernel_name = 'moba_conv_hgrn2_gated_hybrid_step'


def layer_norm(x, g, b):
    xf = x.astype(jnp.float32)
    mu = jnp.mean(xf, axis=-1, keepdims=True)
    var = jnp.mean(jnp.square(xf - mu), axis=-1, keepdims=True)
    return ((xf - mu) * lax.rsqrt(var + LN_EPS) * g + b).astype(x.dtype)


def rms_norm(x, g):
    xf = x.astype(jnp.float32)
    return (xf * lax.rsqrt(jnp.mean(jnp.square(xf), axis=-1, keepdims=True) + LN_EPS) * g).astype(x.dtype)


def to_blocks(a):
    b, l, h, dh = a.shape
    nb = max(-(-l // MOBA_BLOCK), MOBA_TOPK)
    a = jnp.pad(a, ((0, 0), (0, nb * MOBA_BLOCK - l), (0, 0), (0, 0)))
    return a.reshape(b, nb, MOBA_BLOCK, h, dh).transpose(0, 3, 1, 2, 4)


def moba_attend(q, kb, vb, kmean, q_pos):
    b, t, h, dh = q.shape
    nb = kb.shape[2]
    qf = q.astype(jnp.float32)
    own = q_pos // MOBA_BLOCK
    gate = jnp.einsum('bthd,bhnd->bthn', qf, kmean)
    fully_past = jnp.arange(nb)[None, :] < own[:, None]
    gate = jnp.where(fully_past[None, :, None, :], gate, NEG_INF)
    _, top_idx = lax.top_k(gate, MOBA_TOPK)
    top_ok = top_idx < own[None, :, None, None]
    own_idx = jnp.broadcast_to(own[None, :, None, None], (b, t, h, 1)).astype(top_idx.dtype)
    idx = jnp.concatenate([top_idx, own_idx], axis=-1)
    blk_ok = jnp.concatenate([top_ok, jnp.ones((b, t, h, 1), bool)], axis=-1)
    bi = jnp.arange(b)[:, None, None, None]
    hi = jnp.arange(h)[None, None, :, None]
    k_sel = kb[bi, hi, idx].astype(jnp.float32)
    v_sel = vb[bi, hi, idx].astype(jnp.float32)
    key_pos = idx[..., None] * MOBA_BLOCK + jnp.arange(MOBA_BLOCK)
    ok = blk_ok[..., None] & (key_pos <= q_pos[None, :, None, None, None])
    s = jnp.einsum('bthd,bthjkd->bthjk', qf, k_sel) * (ATT_HEAD_DIM ** -0.5)
    s = jnp.where(ok, s, NEG_INF)
    p = jax.nn.softmax(s.reshape(b, t, h, -1), axis=-1).reshape(s.shape)
    o = jnp.einsum('bthjk,bthjkd->bthd', p, v_sel)
    return o.astype(q.dtype)


def moba_prompt(q, k, v):
    b, s, h, dh = q.shape
    kb, vb = to_blocks(k), to_blocks(v)
    kmean = jnp.mean(kb.astype(jnp.float32), axis=3)
    nc = s // MOBA_Q_CHUNK
    qc = q.reshape(b, nc, MOBA_Q_CHUNK, h, dh).transpose(1, 0, 2, 3, 4)
    pos = jnp.arange(s, dtype=jnp.int32).reshape(nc, MOBA_Q_CHUNK)
    out = lax.map(lambda a: moba_attend(a[0], kb, vb, kmean, a[1]), (qc, pos))
    return out.transpose(1, 0, 2, 3, 4).reshape(b, s, h, dh)


def short_conv(u, prev, w, b):
    t = u.shape[1]
    ext = jnp.concatenate([prev.astype(u.dtype), u], axis=1)
    y = b + sum(ext[:, j:j + t] * w[j] for j in range(CONV_K))
    return y, ext[:, t:]


def hgrn_chunk(s_state, inp):
    q, k, lf, v = inp
    c = q.shape[2]
    g = jnp.cumsum(lf, axis=2)
    causal = jnp.tril(jnp.ones((c, c), bool))
    rel = g[:, :, :, None, :] - g[:, :, None, :, :]
    decay = jnp.exp(jnp.where(causal[:, :, None], rel, -jnp.inf))
    a = jnp.einsum('bhtk,bhsk,bhtsk->bhts', q, k, decay)
    o = jnp.einsum('bhtk,bhkv->bhtv', q * jnp.exp(g), s_state) + jnp.einsum('bhts,bhsv->bhtv', a, v)
    g_last = g[:, :, -1]
    s_new = jnp.exp(g_last)[..., None] * s_state + jnp.einsum(
        'bhsk,bhsv->bhkv', k * jnp.exp(g_last[:, :, None] - g), v)
    return s_new, o


def hgrn2(q, f_pre, v, lb, s0):
    b, t, _ = q.shape
    c = math.gcd(t, HG_CHUNK)
    nc = t // c

    def heads(a, d):
        return a.astype(jnp.float32).reshape(b, nc, c, HG_HEADS, d).transpose(1, 0, 3, 2, 4)

    f = lb + (1.0 - lb) * jax.nn.sigmoid(f_pre.astype(jnp.float32))
    xs = (heads(q, HG_DK) * (HG_DK ** -0.5), heads(1.0 - f, HG_DK), heads(jnp.log(f), HG_DK), heads(v, HG_DV))
    s_fin, o = lax.scan(hgrn_chunk, s0.astype(jnp.float32), xs)
    o = o.transpose(1, 0, 3, 2, 4).reshape(b, t, HG_HEADS, HG_DV)
    return o.astype(q.dtype), s_fin.astype(s0.dtype)


def run_trunk(x, c, attn_fn, conv_prev, hg_prev, p):
    b, t, _ = x.shape
    lb_cum = jnp.cumsum(jax.nn.softmax(p['hg_lb_logits'].astype(jnp.float32), axis=0), axis=0)
    lb_all = lb_cum - lb_cum[0]
    x = layer_norm(x, p['ln_in_g'], p['ln_in_b'])
    ks, vs, convs, hgs = [], [], [], []
    for l in range(DEPTH):
        mod = jax.nn.silu(c) @ p['w_ada'][l] + p['b_ada'][l]
        sh_m, sc_m, gt_m, sh_f, sc_f, gt_f = jnp.split(mod[:, None, :], 6, axis=-1)
        u = x * (1 + sc_m) + sh_m
        proj = u @ p['w_mix_in'][l]
        aq, ak, av, cb, cc, ch, hq, hf, hv, hg, gates = jnp.split(proj, IN_SPLITS, axis=-1)
        ak_h = ak.reshape(b, t, ATT_HEADS, ATT_HEAD_DIM)
        av_h = av.reshape(b, t, ATT_HEADS, ATT_HEAD_DIM)
        ya = attn_fn(aq.reshape(b, t, ATT_HEADS, ATT_HEAD_DIM), ak_h, av_h, l).reshape(b, t, ATT_WIDTH)
        z, conv_new = short_conv(cc * ch, conv_prev[l], p['conv_w'][l], p['conv_b'][l])
        yc = cb * z
        o, hg_new = hgrn2(hq, hf, hv, lb_all[l], hg_prev[l])
        yh = (rms_norm(o, p['hg_norm_g'][l]) * jax.nn.silu(hg.reshape(b, t, HG_HEADS, HG_DV))).reshape(b, t, HG_VWIDTH)
        g_a, g_c, g_h = jnp.split(jax.nn.sigmoid(gates), N_BRANCH, axis=-1)
        merged = (g_a * (ya @ p['w_br_attn'][l]) + g_c * (yc @ p['w_br_conv'][l])
                  + g_h * (yh @ p['w_br_hgrn'][l]))
        x = layer_norm(DEEPNORM_ALPHA * x + gt_m * (merged @ p['w_mix_out'][l]), p['ln_m_g'][l], p['ln_m_b'][l])
        u = x * (1 + sc_f) + sh_f
        fa, fb = jnp.split(u @ p['w_ffn_in'][l], 2, axis=-1)
        ff = (jax.nn.silu(fa) * fb) @ p['w_ffn_out'][l]
        x = layer_norm(DEEPNORM_ALPHA * x + gt_f * ff, p['ln_f_g'][l], p['ln_f_b'][l])
        ks.append(ak_h)
        vs.append(av_h)
        convs.append(conv_new)
        hgs.append(hg_new)
    return x, jnp.stack(ks), jnp.stack(vs), jnp.stack(convs), jnp.stack(hgs)


def setup_inputs(seed: int = 0) -> dict:
    key = jax.random.key(seed)
    ks = jax.random.split(key, 32)
    f32 = jnp.float32

    def nrm(k, shape, s=1.0):
        return jax.random.normal(k, shape, f32) * s

    n_pages = PAST_LEN // PAGE_SIZE
    n_used = DEC_BATCH * n_pages
    n_pool = n_used + max(1, n_used // 4)
    page_table = jax.random.permutation(ks[6], n_pool)[:n_used].reshape(DEC_BATCH, n_pages).astype(jnp.int32)
    d = D_MODEL
    beta = DEEPNORM_BETA
    return {
        'x_prompt': nrm(ks[0], (BATCH, SEQ, d)),
        'x_sample': nrm(ks[1], (DEC_BATCH, DEC_SEQ, d)),
        'cache_k': nrm(ks[2], (DEPTH, n_pool, PAGE_SIZE, ATT_HEADS, ATT_HEAD_DIM)),
        'cache_v': nrm(ks[3], (DEPTH, n_pool, PAGE_SIZE, ATT_HEADS, ATT_HEAD_DIM)),
        'state_conv': nrm(ks[4], (DEPTH, DEC_BATCH, CONV_K - 1, CONV_WIDTH)),
        'state_hgrn': nrm(ks[5], (DEPTH, DEC_BATCH, HG_HEADS, HG_DK, HG_DV), 0.3),
        'page_table': page_table,
        'c_prompt': nrm(ks[7], (BATCH, d)),
        'c_sample': nrm(ks[8], (DEC_BATCH, d)),
        'ln_in_g': 1.0 + nrm(ks[9], (d,), 0.02),
        'ln_in_b': nrm(ks[10], (d,), 0.02),
        'w_ada': nrm(ks[11], (DEPTH, d, 6 * d), d ** -0.5),
        'b_ada': nrm(ks[12], (DEPTH, 6 * d), 0.02),
        'w_mix_in': nrm(ks[13], (DEPTH, d, N_IN), d ** -0.5),
        'conv_w': nrm(ks[14], (DEPTH, CONV_K, CONV_WIDTH), CONV_K ** -0.5),
        'conv_b': nrm(ks[15], (DEPTH, CONV_WIDTH), 0.02),
        'hg_lb_logits': nrm(ks[16], (DEPTH, HG_KWIDTH), 0.5),
        'hg_norm_g': 1.0 + nrm(ks[17], (DEPTH, HG_DV), 0.02),
        'w_br_attn': nrm(ks[18], (DEPTH, ATT_WIDTH, d), beta * ATT_WIDTH ** -0.5),
        'w_br_conv': nrm(ks[19], (DEPTH, CONV_WIDTH, d), beta * CONV_WIDTH ** -0.5),
        'w_br_hgrn': nrm(ks[20], (DEPTH, HG_VWIDTH, d), beta * HG_VWIDTH ** -0.5),
        'w_mix_out': nrm(ks[21], (DEPTH, d, d), beta * d ** -0.5),
        'ln_m_g': 1.0 + nrm(ks[22], (DEPTH, d), 0.02),
        'ln_m_b': nrm(ks[23], (DEPTH, d), 0.02),
        'w_ffn_in': nrm(ks[24], (DEPTH, d, 2 * D_FF), beta * d ** -0.5),
        'w_ffn_out': nrm(ks[25], (DEPTH, D_FF, d), beta * D_FF ** -0.5),
        'ln_f_g': 1.0 + nrm(ks[26], (DEPTH, d), 0.02),
        'ln_f_b': nrm(ks[27], (DEPTH, d), 0.02),
    }


def reference(x_prompt, x_sample, cache_k, cache_v, state_conv, state_hgrn, page_table, c_prompt, c_sample,
              ln_in_g, ln_in_b, w_ada, b_ada, w_mix_in, conv_w, conv_b, hg_lb_logits, hg_norm_g,
              w_br_attn, w_br_conv, w_br_hgrn, w_mix_out, ln_m_g, ln_m_b, w_ffn_in, w_ffn_out, ln_f_g, ln_f_b):
    p = {'ln_in_g': ln_in_g, 'ln_in_b': ln_in_b, 'w_ada': w_ada, 'b_ada': b_ada, 'w_mix_in': w_mix_in,
         'conv_w': conv_w, 'conv_b': conv_b, 'hg_lb_logits': hg_lb_logits, 'hg_norm_g': hg_norm_g,
         'w_br_attn': w_br_attn, 'w_br_conv': w_br_conv, 'w_br_hgrn': w_br_hgrn, 'w_mix_out': w_mix_out,
         'ln_m_g': ln_m_g, 'ln_m_b': ln_m_b, 'w_ffn_in': w_ffn_in, 'w_ffn_out': w_ffn_out,
         'ln_f_g': ln_f_g, 'ln_f_b': ln_f_b}

    bp = x_prompt.shape[0]
    conv0 = jnp.zeros((DEPTH, bp, CONV_K - 1, CONV_WIDTH), x_prompt.dtype)
    hg0 = jnp.zeros((DEPTH, bp, HG_HEADS, HG_DK, HG_DV), jnp.float32)
    y_prompt, k_prompt, v_prompt, conv_prompt, hgrn_prompt = run_trunk(
        x_prompt, c_prompt, lambda q, k, v, l: moba_prompt(q, k, v), conv0, hg0, p)

    def attn_sample(q, k, v, l):
        db, t, h, dh = k.shape
        k_all = jnp.concatenate([cache_k[l][page_table].reshape(db, -1, h, dh).astype(k.dtype), k], axis=1)
        v_all = jnp.concatenate([cache_v[l][page_table].reshape(db, -1, h, dh).astype(v.dtype), v], axis=1)
        kb, vb = to_blocks(k_all), to_blocks(v_all)
        kmean = jnp.mean(kb.astype(jnp.float32), axis=3)
        q_pos = PAST_LEN + jnp.arange(t, dtype=jnp.int32)
        return moba_attend(q, kb, vb, kmean, q_pos)

    y_sample, k_sample, v_sample, conv_sample, hgrn_sample = run_trunk(
        x_sample, c_sample, attn_sample, state_conv, state_hgrn, p)

    return (y_prompt, y_sample, k_prompt, v_prompt, conv_prompt, hgrn_prompt,
            k_sample, v_sample, conv_sample, hgrn_sample)
```

```python
import functools
import math

import jax
import jax.numpy as jnp
from jax import lax
from jax.experimental import pallas as pl
from jax.experimental.pallas import tpu as pltpu

F32 = jnp.float32
BF16 = jnp.bfloat16

ATT_HEADS = 8
ATT_HEAD_DIM = 64
ATT_WIDTH = ATT_HEADS * ATT_HEAD_DIM
MOBA_BLOCK = 256
MOBA_TOPK = 3
CONV_WIDTH = 512
HG_HEADS = 4
HG_DK = 128
HG_DV = 128
HG_WIDTH = HG_HEADS * HG_DK
PAGE_SIZE = 128
LN_EPS = 1e-5
NEG_INF = -1e30
ATT_SCALE = ATT_HEAD_DIM ** -0.5
HG_SCALE = HG_DK ** -0.5

LANES = 128
HEAD_PAIR = LANES // ATT_HEAD_DIM
HG_CHUNK = 32
VMEM_LIMIT = 56 * 1024 * 1024


def _cparams(*sem):
    return pltpu.CompilerParams(dimension_semantics=sem, vmem_limit_bytes=VMEM_LIMIT)


def _mm(a, b):
    return jnp.dot(a, b, preferred_element_type=F32)


def _mm_nt(a, b):
    return lax.dot_general(a, b, (((1,), (1,)), ((), ())), preferred_element_type=F32)


def _mm_tn(a, b):
    return lax.dot_general(a, b, (((0,), (0,)), ((), ())), preferred_element_type=F32)


def _split2(x):
    hi = x.astype(BF16)
    lo = (x - hi.astype(F32)).astype(BF16)
    return hi, lo


def _mm_nt_precise(a, b):
    ah, al = _split2(a)
    bh, bl = _split2(b)
    return _mm_nt(ah, bh) + (_mm_nt(ah, bl) + _mm_nt(al, bh))


def _sigmoid(x):
    return 1.0 / (1.0 + jnp.exp(-x))


def _silu(x):
    return x * _sigmoid(x)


def _layer_norm(x, g, b):
    mu = jnp.mean(x, axis=-1, keepdims=True)
    xc = x - mu
    var = jnp.mean(xc * xc, axis=-1, keepdims=True)
    return xc * lax.rsqrt(var + LN_EPS) * g + b


def _modulate(x, sc_ref, sh_ref):
    return (x * (1.0 + sc_ref[...]) + sh_ref[...]).astype(BF16)


def _cumsum_rows(x):
    n = x.shape[0]
    row = lax.broadcasted_iota(jnp.int32, x.shape, 0)
    d = 1
    while d < n:
        x = x + jnp.where(row >= d, pltpu.roll(x, d, 0), 0.0)
        d *= 2
    return x


def _const_spec(shape):
    nd = len(shape)
    return pl.BlockSpec(shape, lambda *_: (0,) * nd)


def _ln_kernel(x_ref, g_ref, b_ref, o_ref):
    o_ref[...] = _layer_norm(x_ref[...], g_ref[...], b_ref[...])


def _ln_call(x2, g, b, tm):
    n, d = x2.shape
    return pl.pallas_call(
        _ln_kernel,
        grid=(n // tm,),
        in_specs=[pl.BlockSpec((tm, d), lambda i: (i, 0)), _const_spec((1, d)), _const_spec((1, d))],
        out_specs=pl.BlockSpec((tm, d), lambda i: (i, 0)),
        out_shape=jax.ShapeDtypeStruct((n, d), F32),
        compiler_params=_cparams("parallel"),
        name="ln_in",
    )(x2, g.reshape(1, d), b.reshape(1, d))


def _ada_kernel(c_ref, w_ref, b_ref, o_ref):
    a = _silu(c_ref[...]).astype(BF16)
    o_ref[...] = _mm(a, w_ref[...].astype(BF16)) + b_ref[...]


def _ada_call(c_all, w_ada, b_ada):
    depth, d, n6 = w_ada.shape
    mp = c_all.shape[0]
    tn = n6 // 4
    return pl.pallas_call(
        _ada_kernel,
        grid=(depth, n6 // tn),
        in_specs=[pl.BlockSpec((mp, d), lambda l, j: (0, 0)),
                  pl.BlockSpec((None, d, tn), lambda l, j: (l, 0, j)),
                  pl.BlockSpec((None, 1, tn), lambda l, j: (l, 0, j))],
        out_specs=pl.BlockSpec((None, mp, tn), lambda l, j: (l, 0, j)),
        out_shape=jax.ShapeDtypeStruct((depth, mp, n6), F32),
        compiler_params=_cparams("parallel", "parallel"),
        name="ada_mod",
    )(c_all, w_ada, b_ada.reshape(depth, 1, n6))


def _mod_spec(rows, d, chunk):
    return pl.BlockSpec((None, rows, d), lambda b, t: (b, 0, chunk))


def _qkv_kernel(x_ref, sc_ref, sh_ref, w_ref, wvt_ref,
                q_ref, qf_ref, k_ref, v_ref, kb_ref, vt_ref, km_ref):
    tm = x_ref.shape[0]
    u = _modulate(x_ref[...], sc_ref, sh_ref)
    p = _mm(u, w_ref[...])
    q = p[:, :ATT_WIDTH]
    k = p[:, ATT_WIDTH:2 * ATT_WIDTH]
    v = p[:, 2 * ATT_WIDTH:]
    q_ref[...] = (q * ATT_SCALE).astype(BF16)
    qf_ref[...] = q
    k_ref[...] = k
    v_ref[...] = v
    kb_ref[...] = k.astype(BF16)
    vt = _mm_nt(wvt_ref[...], u).astype(BF16)
    for i in range(tm // MOBA_BLOCK):
        rows = slice(i * MOBA_BLOCK, (i + 1) * MOBA_BLOCK)
        vt_ref[i] = vt[:, rows]
        km_ref[i] = jnp.sum(k[rows], axis=0, keepdims=True) * (1.0 / MOBA_BLOCK)


def _qkv_call(x3, mod3, w_qkv, w_vt, tm):
    b, s, d = x3.shape
    nbt = tm // MOBA_BLOCK
    nb = s // MOBA_BLOCK
    row = lambda bb, t: (bb, t, 0)
    act = pl.BlockSpec((None, tm, ATT_WIDTH), row)
    return pl.pallas_call(
        _qkv_kernel,
        grid=(b, s // tm),
        in_specs=[pl.BlockSpec((None, tm, d), row), _mod_spec(1, d, 1), _mod_spec(1, d, 0),
                  _const_spec(w_qkv.shape), _const_spec(w_vt.shape)],
        out_specs=[act, act, act, act, act,
                   pl.BlockSpec((None, nbt, ATT_WIDTH, MOBA_BLOCK), lambda bb, t: (bb, t, 0, 0)),
                   pl.BlockSpec((None, nbt, 1, ATT_WIDTH), lambda bb, t: (bb, t, 0, 0))],
        out_shape=[jax.ShapeDtypeStruct((b, s, ATT_WIDTH), BF16),
                   jax.ShapeDtypeStruct((b, s, ATT_WIDTH), F32),
                   jax.ShapeDtypeStruct((b, s, ATT_WIDTH), F32),
                   jax.ShapeDtypeStruct((b, s, ATT_WIDTH), F32),
                   jax.ShapeDtypeStruct((b, s, ATT_WIDTH), BF16),
                   jax.ShapeDtypeStruct((b, nb, ATT_WIDTH, MOBA_BLOCK), BF16),
                   jax.ShapeDtypeStruct((b, nb, 1, ATT_WIDTH), F32)],
        compiler_params=_cparams("parallel", "parallel"),
        name="proj_qkv",
    )(x3, mod3, mod3, w_qkv, w_vt)


def _conv_kernel(x_ref, sc_ref, sh_ref, w_ref, cw_ref, cb_ref, y_ref, st_ref, carry_ref):
    tm = x_ref.shape[0]
    t = pl.program_id(1)

    @pl.when(t == 0)
    def _():
        carry_ref[...] = jnp.zeros_like(carry_ref)

    u = _modulate(x_ref[...], sc_ref, sh_ref)
    p = _mm(u, w_ref[...])
    gate_b = p[:, :CONV_WIDTH]
    ucv = p[:, CONV_WIDTH:2 * CONV_WIDTH] * p[:, 2 * CONV_WIDTH:]
    prev1 = carry_ref[7:8, :]
    prev2 = carry_ref[6:7, :]
    row = lax.broadcasted_iota(jnp.int32, ucv.shape, 0)
    s1 = jnp.where(row == 0, prev1, pltpu.roll(ucv, 1, 0))
    s2 = jnp.where(row == 0, prev2, jnp.where(row == 1, prev1, pltpu.roll(ucv, 2, 0)))
    cw = cw_ref[...]
    z = cb_ref[...] + s2 * cw[0:1, :] + s1 * cw[1:2, :] + ucv * cw[2:3, :]
    y_ref[...] = (gate_b * z).astype(BF16)
    carry_ref[...] = ucv[tm - 8:, :]

    @pl.when(t == pl.num_programs(1) - 1)
    def _():
        st_ref[...] = ucv[tm - 2:, :]


def _conv_call(x3, mod3, w_conv, conv_w, conv_b, tm):
    b, s, d = x3.shape
    row = lambda bb, t: (bb, t, 0)
    return pl.pallas_call(
        _conv_kernel,
        grid=(b, s // tm),
        in_specs=[pl.BlockSpec((None, tm, d), row), _mod_spec(1, d, 1), _mod_spec(1, d, 0),
                  _const_spec(w_conv.shape), _const_spec(conv_w.shape), _const_spec((1, CONV_WIDTH))],
        out_specs=[pl.BlockSpec((None, tm, CONV_WIDTH), row),
                   pl.BlockSpec((None, 2, CONV_WIDTH), lambda bb, t: (bb, 0, 0))],
        out_shape=[jax.ShapeDtypeStruct((b, s, CONV_WIDTH), BF16),
                   jax.ShapeDtypeStruct((b, 2, CONV_WIDTH), F32)],
        scratch_shapes=[pltpu.VMEM((8, CONV_WIDTH), F32)],
        compiler_params=_cparams("parallel", "arbitrary"),
        name="proj_conv",
    )(x3, mod3, mod3, w_conv, conv_w, conv_b.reshape(1, CONV_WIDTH))


def _hgproj_kernel(x_ref, sc_ref, sh_ref, w_ref, lb_ref, q_ref, lf_ref, v_ref, g_ref):
    u = _modulate(x_ref[...], sc_ref, sh_ref)
    p = _mm(u, w_ref[...])
    lb = lb_ref[...]
    q_ref[...] = p[:, :HG_WIDTH] * HG_SCALE
    f = lb + (1.0 - lb) * _sigmoid(p[:, HG_WIDTH:2 * HG_WIDTH])
    lf_ref[...] = jnp.log(f)
    v_ref[...] = p[:, 2 * HG_WIDTH:3 * HG_WIDTH].astype(BF16)
    g_ref[...] = p[:, 3 * HG_WIDTH:]


def _hgproj_call(x3, mod3, w_hg, lb, tm):
    b, s, d = x3.shape
    row = lambda bb, t: (bb, t, 0)
    act = pl.BlockSpec((None, tm, HG_WIDTH), row)
    return pl.pallas_call(
        _hgproj_kernel,
        grid=(b, s // tm),
        in_specs=[pl.BlockSpec((None, tm, d), row), _mod_spec(1, d, 1), _mod_spec(1, d, 0),
                  _const_spec(w_hg.shape), _const_spec((1, HG_WIDTH))],
        out_specs=[act, act, act, act],
        out_shape=[jax.ShapeDtypeStruct((b, s, HG_WIDTH), F32),
                   jax.ShapeDtypeStruct((b, s, HG_WIDTH), F32),
                   jax.ShapeDtypeStruct((b, s, HG_WIDTH), BF16),
                   jax.ShapeDtypeStruct((b, s, HG_WIDTH), F32)],
        compiler_params=_cparams("parallel", "parallel"),
        name="proj_hgrn",
    )(x3, mod3, mod3, w_hg, lb.reshape(1, HG_WIDTH))


def _gates_kernel(x_ref, sc_ref, sh_ref, w_ref, o_ref):
    u = _modulate(x_ref[...], sc_ref, sh_ref)
    o_ref[...] = _sigmoid(_mm(u, w_ref[...])).astype(BF16)


def _gates_call(x3, mod3, w_g, tm):
    b, s, d = x3.shape
    n = w_g.shape[1]
    row = lambda bb, t: (bb, t, 0)
    return pl.pallas_call(
        _gates_kernel,
        grid=(b, s // tm),
        in_specs=[pl.BlockSpec((None, tm, d), row), _mod_spec(1, d, 1), _mod_spec(1, d, 0),
                  _const_spec(w_g.shape)],
        out_specs=pl.BlockSpec((None, tm, n), row),
        out_shape=jax.ShapeDtypeStruct((b, s, n), BF16),
        compiler_params=_cparams("parallel", "parallel"),
        name="proj_gates",
    )(x3, mod3, mod3, w_g)


def _softmax_step(state, s, vt_h):
    m, l, acc = state
    m_new = jnp.maximum(m, jnp.max(s, axis=0, keepdims=True))
    alpha = jnp.exp(m - m_new)
    p = jnp.exp(s - m_new)
    l = alpha * l + jnp.sum(p, axis=0, keepdims=True)
    acc = alpha * acc + _mm(vt_h, p.astype(BF16))
    return m_new, l, acc


def _moba_kernel(q_ref, qf_ref, kb_ref, vt_ref, km_ref, o_ref, bias_ref):
    nbp = km_ref.shape[0]
    blk = MOBA_BLOCK
    qi = pl.program_id(2)
    q = q_ref[...]
    qf = qf_ref[...]
    km = km_ref[...]
    lane = lax.broadcasted_iota(jnp.int32, (blk, LANES), 1)
    blk_id = lax.broadcasted_iota(jnp.int32, (nbp, blk), 0)
    past = blk_id < qi
    qs = []
    for h in range(HEAD_PAIR):
        in_head = (lane >= h * ATT_HEAD_DIM) & (lane < (h + 1) * ATT_HEAD_DIM)
        qs.append(jnp.where(in_head, q, jnp.zeros_like(q)))
        gate = _mm_nt_precise(km, jnp.where(in_head, qf, 0.0))
        gate = jnp.where(past, gate, NEG_INF)
        rank = jnp.zeros((nbp, blk), jnp.int32)
        for jp in range(nbp):
            gj = gate[jp:jp + 1, :]
            beats = (gj > gate) | ((gj == gate) & (blk_id > jp))
            rank = rank + beats.astype(jnp.int32)
        sel = past & (rank < MOBA_TOPK)
        bias_ref[h] = jnp.where(sel, 0.0, NEG_INF)

    def body(j, carry):
        kj = kb_ref[j]
        vtj = vt_ref[j]
        out = []
        for h in range(HEAD_PAIR):
            s = _mm_nt(kj, qs[h]) + bias_ref[h, pl.ds(j, 1), :]
            out.append(_softmax_step(carry[h], s, vtj[h * ATT_HEAD_DIM:(h + 1) * ATT_HEAD_DIM, :]))
        return tuple(out)

    init = tuple((jnp.full((1, blk), NEG_INF, F32), jnp.zeros((1, blk), F32),
                  jnp.zeros((ATT_HEAD_DIM, blk), F32)) for _ in range(HEAD_PAIR))
    carry = lax.fori_loop(0, qi, body, init)

    kj = kb_ref[qi]
    vtj = vt_ref[qi]
    key = lax.broadcasted_iota(jnp.int32, (blk, blk), 0)
    qry = lax.broadcasted_iota(jnp.int32, (blk, blk), 1)
    causal = jnp.where(key <= qry, 0.0, NEG_INF)
    outs = []
    for h in range(HEAD_PAIR):
        s = _mm_nt(kj, qs[h]) + causal
        _, l, acc = _softmax_step(carry[h], s, vtj[h * ATT_HEAD_DIM:(h + 1) * ATT_HEAD_DIM, :])
        outs.append(acc / l)
    o_ref[...] = jnp.concatenate(outs, axis=0).T.astype(BF16)


def _moba_call(q, qf, kb4, vt4, km3):
    b, s, _ = q.shape
    nb = s // MOBA_BLOCK
    nbp = km3.shape[1]
    npair = ATT_HEADS // HEAD_PAIR
    qspec = pl.BlockSpec((None, MOBA_BLOCK, LANES), lambda bb, hp, qi: (bb, qi, hp))
    return pl.pallas_call(
        _moba_kernel,
        grid=(b, npair, nb),
        in_specs=[qspec, qspec,
                  pl.BlockSpec((None, nb, MOBA_BLOCK, LANES), lambda bb, hp, qi: (bb, 0, 0, hp)),
                  pl.BlockSpec((None, nb, LANES, MOBA_BLOCK), lambda bb, hp, qi: (bb, 0, hp, 0)),
                  pl.BlockSpec((None, nbp, LANES), lambda bb, hp, qi: (bb, 0, hp))],
        out_specs=qspec,
        out_shape=jax.ShapeDtypeStruct((b, s, ATT_WIDTH), BF16),
        scratch_shapes=[pltpu.VMEM((HEAD_PAIR, nbp, MOBA_BLOCK), F32)],
        compiler_params=_cparams("parallel", "parallel", "arbitrary"),
        name="moba_prompt",
    )(q, qf, kb4, vt4, km3)


def _hgrn_kernel(q_ref, lf_ref, v_ref, g_ref, gn_ref, y_ref, so_ref, st_ref):
    th = q_ref.shape[0]
    c = HG_CHUNK
    t = pl.program_id(1)

    @pl.when(t == 0)
    def _():
        st_ref[...] = jnp.zeros_like(st_ref)

    tril = (lax.broadcasted_iota(jnp.int32, (c, c), 0) >= lax.broadcasted_iota(jnp.int32, (c, c), 1))
    gn = gn_ref[...]
    for ci in range(th // c):
        rows = slice(ci * c, (ci + 1) * c)
        for h in range(HG_HEADS):
            lanes = slice(h * HG_DK, (h + 1) * HG_DK)
            lf = lf_ref[rows, lanes]
            q = q_ref[rows, lanes]
            v = v_ref[rows, lanes]
            g = _cumsum_rows(lf)
            gmid = g[c // 2 - 1:c // 2, :]
            glast = g[c - 1:c, :]
            k = 1.0 - jnp.exp(lf)
            a = _mm_nt((q * jnp.exp(g - gmid)).astype(BF16), (k * jnp.exp(gmid - g)).astype(BF16))
            a = jnp.where(tril, a, 0.0)
            st = st_ref[h]
            o = (_mm_nt((q * jnp.exp(g)).astype(BF16), st.astype(BF16))
                 + _mm(a.astype(BF16), v))
            kd = (k * jnp.exp(glast - g)).astype(BF16)
            st_ref[h] = st * jnp.exp(glast) + _mm_tn(v, kd)
            rms = lax.rsqrt(jnp.mean(o * o, axis=-1, keepdims=True) + LN_EPS)
            y_ref[rows, lanes] = (o * rms * gn * _silu(g_ref[rows, lanes])).astype(BF16)

    @pl.when(t == pl.num_programs(1) - 1)
    def _():
        for h in range(HG_HEADS):
            so_ref[h] = st_ref[h].T


def _hgrn_call(q, lf, v, g, gn, th):
    b, s, _ = q.shape
    row = lambda bb, t: (bb, t, 0)
    act = pl.BlockSpec((None, th, HG_WIDTH), row)
    return pl.pallas_call(
        _hgrn_kernel,
        grid=(b, s // th),
        in_specs=[act, act, act, act, _const_spec((1, HG_DV))],
        out_specs=[act, pl.BlockSpec((None, HG_HEADS, HG_DK, HG_DV), lambda bb, t: (bb, 0, 0, 0))],
        out_shape=[jax.ShapeDtypeStruct((b, s, HG_WIDTH), BF16),
                   jax.ShapeDtypeStruct((b, HG_HEADS, HG_DK, HG_DV), F32)],
        scratch_shapes=[pltpu.VMEM((HG_HEADS, HG_DV, HG_DK), F32)],
        compiler_params=_cparams("parallel", "arbitrary"),
        name="hgrn_prompt",
    )(q, lf, v, g, gn.reshape(1, HG_DV))


def _merge_kernel(x_ref, ya_ref, yc_ref, yh_ref, gates_ref, gt_ref,
                  wa_ref, wc_ref, wh_ref, wo_ref, lng_ref, lnb_ref, o_ref, *, alpha):
    d = x_ref.shape[1]
    gates = gates_ref[...]
    merged = (gates[:, :d].astype(F32) * _mm(ya_ref[...], wa_ref[...])
              + gates[:, d:2 * d].astype(F32) * _mm(yc_ref[...], wc_ref[...])
              + gates[:, 2 * d:].astype(F32) * _mm(yh_ref[...], wh_ref[...]))
    y = _mm(merged.astype(BF16), wo_ref[...])
    o_ref[...] = _layer_norm(alpha * x_ref[...] + gt_ref[...] * y, lng_ref[...], lnb_ref[...])


def _merge_call(x3, ya, yc, yh, gates, mod3, wa, wc, wh, wo, lng, lnb, tm, alpha):
    b, s, d = x3.shape
    mod_rows = mod3.shape[1]
    mrows = 1 if mod_rows == 1 else tm
    row = lambda bb, t: (bb, t, 0)
    br = pl.BlockSpec((None, tm, ya.shape[2]), row)
    xs = pl.BlockSpec((None, tm, d), row)
    return pl.pallas_call(
        functools.partial(_merge_kernel, alpha=alpha),
        grid=(b, s // tm),
        in_specs=[xs, br, br, br, pl.BlockSpec((None, tm, 3 * d), row), _mod_spec(mrows, d, 2),
                  _const_spec(wa.shape), _const_spec(wc.shape), _const_spec(wh.shape),
                  _const_spec(wo.shape), _const_spec((1, d)), _const_spec((1, d))],
        out_specs=xs,
        out_shape=jax.ShapeDtypeStruct((b, s, d), F32),
        compiler_params=_cparams("parallel", "parallel"),
        name="merge",
    )(x3, ya, yc, yh, gates, mod3, wa, wc, wh, wo, lng.reshape(1, d), lnb.reshape(1, d))


def _ffn_kernel(x_ref, sc_ref, sh_ref, gt_ref, wi_ref, wo_ref, lng_ref, lnb_ref, o_ref, *, alpha):
    x = x_ref[...]
    dff = wo_ref.shape[0]
    h = _mm(_modulate(x, sc_ref, sh_ref), wi_ref[...])
    act = (_silu(h[:, :dff]) * h[:, dff:]).astype(BF16)
    y = _mm(act, wo_ref[...])
    o_ref[...] = _layer_norm(alpha * x + gt_ref[...] * y, lng_ref[...], lnb_ref[...])


def _ffn_call(x3, mod3, wi, wo, lng, lnb, tm, alpha):
    b, s, d = x3.shape
    mod_rows = mod3.shape[1]
    mrows = 1 if mod_rows == 1 else tm
    row = lambda bb, t: (bb, t, 0)
    xs = pl.BlockSpec((None, tm, d), row)
    return pl.pallas_call(
        functools.partial(_ffn_kernel, alpha=alpha),
        grid=(b, s // tm),
        in_specs=[xs, _mod_spec(mrows, d, 4), _mod_spec(mrows, d, 3), _mod_spec(mrows, d, 5),
                  _const_spec(wi.shape), _const_spec(wo.shape), _const_spec((1, d)), _const_spec((1, d))],
        out_specs=xs,
        out_shape=jax.ShapeDtypeStruct((b, s, d), F32),
        compiler_params=_cparams("parallel", "parallel"),
        name="ffn",
    )(x3, mod3, mod3, mod3, wi, wo, lng.reshape(1, d), lnb.reshape(1, d))


def _kmean_kernel(pt_ref, *refs):
    o_ref = refs[-1]
    pages = refs[:-1]
    ppb = MOBA_BLOCK // PAGE_SIZE
    for i in range(len(pages) // ppb):
        acc = jnp.sum(pages[ppb * i][...], axis=0, keepdims=True)
        for r in range(1, ppb):
            acc = acc + jnp.sum(pages[ppb * i + r][...], axis=0, keepdims=True)
        o_ref[i:i + 1, :] = acc * (1.0 / MOBA_BLOCK)


def _kmean_call(cache_k4, page_table, pps):
    depth = cache_k4.shape[0]
    db, n_pages = page_table.shape
    ppb = MOBA_BLOCK // PAGE_SIZE
    nbs = n_pages // ppb

    def page_spec(n):
        return pl.BlockSpec((None, None, PAGE_SIZE, ATT_WIDTH),
                            lambda l, b, c, pt: (l, pt[b, c * pps + n], 0, 0))

    return pl.pallas_call(
        _kmean_kernel,
        grid_spec=pltpu.PrefetchScalarGridSpec(
            num_scalar_prefetch=1,
            grid=(depth, db, n_pages // pps),
            in_specs=[page_spec(n) for n in range(pps)],
            out_specs=pl.BlockSpec((None, None, pps // ppb, ATT_WIDTH), lambda l, b, c, pt: (l, b, c, 0)),
        ),
        out_shape=jax.ShapeDtypeStruct((depth, db, nbs, ATT_WIDTH), F32),
        compiler_params=_cparams("parallel", "parallel", "parallel"),
        name="cache_kmean",
    )(page_table, *([cache_k4] * pps))


def _sproj_kernel(x_ref, sc_ref, sh_ref, w_ref, o_ref):
    o_ref[...] = _mm(_modulate(x_ref[...], sc_ref, sh_ref), w_ref[...])


def _sproj_call(x2, mod2, w_in, tn):
    m, d = x2.shape
    n = w_in.shape[1]
    return pl.pallas_call(
        _sproj_kernel,
        grid=(n // tn,),
        in_specs=[pl.BlockSpec((m, d), lambda j: (0, 0)),
                  pl.BlockSpec((m, d), lambda j: (0, 1)),
                  pl.BlockSpec((m, d), lambda j: (0, 0)),
                  pl.BlockSpec((d, tn), lambda j: (0, j))],
        out_specs=pl.BlockSpec((m, tn), lambda j: (0, j)),
        out_shape=jax.ShapeDtypeStruct((m, n), F32),
        compiler_params=_cparams("parallel"),
        name="sample_proj",
    )(x2, mod2, mod2, w_in)


def _ssel_kernel(q_ref, km_ref, o_ref):
    nbs = km_ref.shape[0]
    prod = km_ref[...] * q_ref[...]
    dim = lax.broadcasted_iota(jnp.int32, (ATT_WIDTH, LANES), 0)
    head = lax.broadcasted_iota(jnp.int32, (ATT_WIDTH, LANES), 1)
    onehot = (dim // ATT_HEAD_DIM == head).astype(BF16)
    hi = prod.astype(BF16)
    r1 = prod - hi.astype(F32)
    mid = r1.astype(BF16)
    lo = (r1 - mid.astype(F32)).astype(BF16)
    gate = _mm(hi, onehot) + (_mm(mid, onehot) + _mm(lo, onehot))
    blk = lax.broadcasted_iota(jnp.int32, (nbs, LANES), 0)
    rows = []
    for _ in range(MOBA_TOPK):
        best = jnp.max(gate, axis=0, keepdims=True)
        idx = jnp.min(jnp.where(gate == best, blk, nbs), axis=0, keepdims=True)
        rows.append(idx)
        gate = jnp.where(blk == idx, -jnp.inf, gate)
    rows.append(jnp.zeros((8 - MOBA_TOPK, LANES), jnp.int32))
    o_ref[...] = jnp.concatenate(rows, axis=0)


def _ssel_call(q3, km_l):
    db, nbs, _ = km_l.shape
    return pl.pallas_call(
        _ssel_kernel,
        grid=(db,),
        in_specs=[pl.BlockSpec((None, 1, ATT_WIDTH), lambda b: (b, 0, 0)),
                  pl.BlockSpec((None, nbs, ATT_WIDTH), lambda b: (b, 0, 0))],
        out_specs=pl.BlockSpec((None, 8, LANES), lambda b: (b, 0, 0)),
        out_shape=jax.ShapeDtypeStruct((db, 8, LANES), jnp.int32),
        compiler_params=_cparams("parallel"),
        name="sample_select",
    )(q3, km_l)


def _sattn_kernel(sel_ref, pt_ref, q_ref, kn_ref, vn_ref, *refs):
    o_ref = refs[-1]
    npg = (len(refs) - 1) // 2
    kpages = refs[:npg]
    vpages = refs[npg:2 * npg]
    per_head = npg // HEAD_PAIR
    lane = lax.broadcasted_iota(jnp.int32, (1, LANES), 1)
    q = q_ref[...] * ATT_SCALE
    kn = kn_ref[...]
    vn = vn_ref[...]
    out = jnp.zeros((1, LANES), F32)
    for h in range(HEAD_PAIR):
        in_head = (lane >= h * ATT_HEAD_DIM) & (lane < (h + 1) * ATT_HEAD_DIM)
        qh = jnp.where(in_head, q, 0.0)
        q8 = jnp.broadcast_to(qh, (8, LANES)).astype(BF16)
        s_self = jnp.sum(qh * kn, axis=-1, keepdims=True)
        scores = [_mm_nt(q8, kpages[h * per_head + i][...].astype(BF16)) for i in range(per_head)]
        m = s_self
        for s in scores:
            m = jnp.maximum(m, jnp.max(s[0:1, :], axis=-1, keepdims=True))
        p_self = jnp.exp(s_self - m)
        l = p_self
        acc = p_self * vn
        for i, s in enumerate(scores):
            p = jnp.exp(s - m)
            l = l + jnp.sum(p[0:1, :], axis=-1, keepdims=True)
            acc = acc + _mm(p.astype(BF16), vpages[h * per_head + i][...].astype(BF16))[0:1, :]
        out = jnp.where(in_head, acc / l, out)
    o_ref[...] = out.astype(BF16)


def _sattn_call(sel_flat, page_table, q3, kn3, vn3, cache_k4, cache_v4, layer):
    db = q3.shape[0]
    ppb = MOBA_BLOCK // PAGE_SIZE
    npair = ATT_HEADS // HEAD_PAIR
    vec = pl.BlockSpec((None, 1, LANES), lambda b, hp, sel, pt: (b, 0, hp))

    def page_spec(h, i, r):
        def imap(b, hp, sel, pt):
            blk = sel[(b * ATT_HEADS + hp * HEAD_PAIR + h) * MOBA_TOPK + i]
            return (layer, pt[b, blk * ppb + r], 0, hp)
        return pl.BlockSpec((None, None, PAGE_SIZE, LANES), imap)

    pages = [page_spec(h, i, r) for h in range(HEAD_PAIR) for i in range(MOBA_TOPK) for r in range(ppb)]
    return pl.pallas_call(
        _sattn_kernel,
        grid_spec=pltpu.PrefetchScalarGridSpec(
            num_scalar_prefetch=2,
            grid=(db, npair),
            in_specs=[vec, vec, vec] + pages + pages,
            out_specs=vec,
        ),
        out_shape=jax.ShapeDtypeStruct((db, 1, ATT_WIDTH), BF16),
        compiler_params=_cparams("parallel", "parallel"),
        name="sample_attn",
    )(sel_flat, page_table, q3, kn3, vn3, *([cache_k4] * len(pages)), *([cache_v4] * len(pages)))


def _sbranch_kernel(p_ref, cs_ref, hs_ref, cw_ref, cb_ref, lb_ref, gn_ref,
                    yc_ref, yh_ref, gates_ref, cso_ref, hso_ref):
    nb = p_ref.shape[0]
    p = p_ref[...]
    o0 = 3 * ATT_WIDTH
    gate_b = p[:, o0:o0 + CONV_WIDTH]
    ucv = p[:, o0 + CONV_WIDTH:o0 + 2 * CONV_WIDTH] * p[:, o0 + 2 * CONV_WIDTH:o0 + 3 * CONV_WIDTH]
    cs = cs_ref[...]
    prev2 = cs[:, :CONV_WIDTH]
    prev1 = cs[:, CONV_WIDTH:]
    cw = cw_ref[...]
    z = cb_ref[...] + prev2 * cw[0:1, :] + prev1 * cw[1:2, :] + ucv * cw[2:3, :]
    yc_ref[...] = (gate_b * z).astype(BF16)
    cso_ref[...] = jnp.concatenate([prev1, ucv], axis=1)

    o1 = o0 + 3 * CONV_WIDTH
    lb = lb_ref[...]
    hq = p[:, o1:o1 + HG_WIDTH] * HG_SCALE
    f = lb + (1.0 - lb) * _sigmoid(p[:, o1 + HG_WIDTH:o1 + 2 * HG_WIDTH])
    hv = p[:, o1 + 2 * HG_WIDTH:o1 + 3 * HG_WIDTH]
    hg = p[:, o1 + 3 * HG_WIDTH:o1 + 4 * HG_WIDTH]
    qt = hq.T
    ft = f.T
    kt = (1.0 - f).T
    gn = gn_ref[...]
    rows = []
    for b in range(nb):
        heads = []
        for h in range(HG_HEADS):
            ks = slice(h * HG_DK, (h + 1) * HG_DK)
            s_new = ft[ks, b:b + 1] * hs_ref[b, h] + kt[ks, b:b + 1] * hv[b:b + 1, ks]
            hso_ref[b, h] = s_new
            o = jnp.sum(qt[ks, b:b + 1] * s_new, axis=0, keepdims=True)
            rms = lax.rsqrt(jnp.mean(o * o, axis=-1, keepdims=True) + LN_EPS)
            heads.append(o * rms * gn)
        rows.append(jnp.concatenate(heads, axis=1))
    yh_ref[...] = (jnp.concatenate(rows, axis=0) * _silu(hg)).astype(BF16)
    gates_ref[...] = _sigmoid(p[:, o1 + 4 * HG_WIDTH:]).astype(BF16)


def _sbranch_call(proj, conv_state2, hg_state, conv_w, conv_b, lb, gn, nb):
    db, n_in = proj.shape
    d3 = n_in - 3 * ATT_WIDTH - 3 * CONV_WIDTH - 4 * HG_WIDTH
    row = lambda g: (g, 0)
    st = pl.BlockSpec((nb, HG_HEADS, HG_DK, HG_DV), lambda g: (g, 0, 0, 0))
    return pl.pallas_call(
        _sbranch_kernel,
        grid=(db // nb,),
        in_specs=[pl.BlockSpec((nb, n_in), row), pl.BlockSpec((nb, 2 * CONV_WIDTH), row), st,
                  _const_spec(conv_w.shape), _const_spec((1, CONV_WIDTH)), _const_spec((1, HG_WIDTH)),
                  _const_spec((1, HG_DV))],
        out_specs=[pl.BlockSpec((nb, CONV_WIDTH), row), pl.BlockSpec((nb, HG_WIDTH), row),
                   pl.BlockSpec((nb, d3), row), pl.BlockSpec((nb, 2 * CONV_WIDTH), row), st],
        out_shape=[jax.ShapeDtypeStruct((db, CONV_WIDTH), BF16),
                   jax.ShapeDtypeStruct((db, HG_WIDTH), BF16),
                   jax.ShapeDtypeStruct((db, d3), BF16),
                   jax.ShapeDtypeStruct((db, 2 * CONV_WIDTH), F32),
                   jax.ShapeDtypeStruct(hg_state.shape, F32)],
        compiler_params=_cparams("parallel"),
        name="sample_branch",
    )(proj, conv_state2, hg_state, conv_w, conv_b.reshape(1, CONV_WIDTH), lb.reshape(1, HG_WIDTH),
      gn.reshape(1, HG_DV))


def kernel(x_prompt, x_sample, cache_k, cache_v, state_conv, state_hgrn, page_table, c_prompt, c_sample,
           ln_in_g, ln_in_b, w_ada, b_ada, w_mix_in, conv_w, conv_b, hg_lb_logits, hg_norm_g,
           w_br_attn, w_br_conv, w_br_hgrn, w_mix_out, ln_m_g, ln_m_b, w_ffn_in, w_ffn_out, ln_f_g, ln_f_b):
    bp, seq, d = x_prompt.shape
    db = x_sample.shape[0]
    depth = w_ada.shape[0]
    n_pages = page_table.shape[1]
    past_blocks = n_pages * PAGE_SIZE // MOBA_BLOCK
    assert x_sample.shape[1] == 1 and seq % MOBA_BLOCK == 0
    assert past_blocks >= MOBA_TOPK and (n_pages * PAGE_SIZE) % MOBA_BLOCK == 0
    alpha = (2 * depth) ** 0.25
    nb = seq // MOBA_BLOCK
    nbp = -(-nb // 8) * 8

    tm = min(512, seq)
    tm_ffn = min(256, seq)
    th = min(128, seq)

    o_conv = 3 * ATT_WIDTH
    o_hg = o_conv + 3 * CONV_WIDTH
    o_gate = o_hg + 4 * HG_WIDTH
    w_in = w_mix_in.astype(BF16)
    w_qkv = w_in[:, :, :o_conv]
    w_vt = jnp.swapaxes(w_in[:, :, 2 * ATT_WIDTH:o_conv], 1, 2)
    w_conv = w_in[:, :, o_conv:o_hg]
    w_hg = w_in[:, :, o_hg:o_gate]
    w_gate = w_in[:, :, o_gate:]
    wa, wc, wh, wo = (w.astype(BF16) for w in (w_br_attn, w_br_conv, w_br_hgrn, w_mix_out))
    wfi, wfo = w_ffn_in.astype(BF16), w_ffn_out.astype(BF16)
    lb_cum = jnp.cumsum(jax.nn.softmax(hg_lb_logits.astype(F32), axis=0), axis=0)
    lb_all = lb_cum - lb_cum[0]

    n_c = bp + db
    mp = -(-n_c // 8) * 8
    c_all = jnp.concatenate([c_prompt, c_sample, jnp.zeros((mp - n_c, d), F32)], axis=0)
    mod = _ada_call(c_all, w_ada, b_ada)

    cache_k4 = cache_k.reshape(depth, cache_k.shape[1], PAGE_SIZE, ATT_WIDTH)
    cache_v4 = cache_v.reshape(depth, cache_v.shape[1], PAGE_SIZE, ATT_WIDTH)
    km_s = _kmean_call(cache_k4, page_table, min(16, n_pages))

    xp = _ln_call(x_prompt.reshape(bp * seq, d), ln_in_g, ln_in_b, tm).reshape(bp, seq, d)
    xs = _ln_call(x_sample.reshape(db, d), ln_in_g, ln_in_b, db)

    ks, vs, convs, hgs = [], [], [], []
    ks_s, vs_s, convs_s, hgs_s = [], [], [], []
    conv_state2 = state_conv.reshape(depth, db, 2 * CONV_WIDTH)
    for l in range(depth):
        mod_p = mod[l, :bp].reshape(bp, 1, 6 * d)
        mod_s = mod[l, bp:bp + db]

        q, qf, k, v, kb, vt, km = _qkv_call(xp, mod_p, w_qkv[l], w_vt[l], tm)
        km3 = km.reshape(bp, nb, ATT_WIDTH)
        if nbp != nb:
            km3 = jnp.pad(km3, ((0, 0), (0, nbp - nb), (0, 0)))
        ya = _moba_call(q, qf, kb.reshape(bp, nb, MOBA_BLOCK, ATT_WIDTH), vt, km3)
        yc, conv_new = _conv_call(xp, mod_p, w_conv[l], conv_w[l], conv_b[l], tm)
        hq, hlf, hv, hg = _hgproj_call(xp, mod_p, w_hg[l], lb_all[l], tm)
        yh, hg_new = _hgrn_call(hq, hlf, hv, hg, hg_norm_g[l], th)
        gates = _gates_call(xp, mod_p, w_gate[l], tm)
        xp = _merge_call(xp, ya, yc, yh, gates, mod_p, wa[l], wc[l], wh[l], wo[l],
                         ln_m_g[l], ln_m_b[l], tm, alpha)
        xp = _ffn_call(xp, mod_p, wfi[l], wfo[l], ln_f_g[l], ln_f_b[l], tm_ffn, alpha)
        ks.append(k.reshape(bp, seq, ATT_HEADS, ATT_HEAD_DIM))
        vs.append(v.reshape(bp, seq, ATT_HEADS, ATT_HEAD_DIM))
        convs.append(conv_new)
        hgs.append(hg_new)

        proj = _sproj_call(xs, mod_s, w_in[l], 1024)
        q3 = proj[:, :ATT_WIDTH].reshape(db, 1, ATT_WIDTH)
        kn = proj[:, ATT_WIDTH:2 * ATT_WIDTH]
        vn = proj[:, 2 * ATT_WIDTH:o_conv]
        sel = _ssel_call(q3, km_s[l])
        sel_flat = jnp.swapaxes(sel[:, :MOBA_TOPK, :ATT_HEADS], 1, 2).reshape(-1)
        ya_s = _sattn_call(sel_flat, page_table, q3, kn.reshape(db, 1, ATT_WIDTH),
                           vn.reshape(db, 1, ATT_WIDTH), cache_k4, cache_v4, l)
        yc_s, yh_s, gates_s, conv_s, hg_s = _sbranch_call(
            proj, conv_state2[l], state_hgrn[l], conv_w[l], conv_b[l], lb_all[l], hg_norm_g[l], min(8, db))
        xs3 = xs.reshape(1, db, d)
        mod_s3 = mod_s.reshape(1, db, 6 * d)
        xs3 = _merge_call(xs3, ya_s.reshape(1, db, ATT_WIDTH), yc_s.reshape(1, db, CONV_WIDTH),
                          yh_s.reshape(1, db, HG_WIDTH), gates_s.reshape(1, db, 3 * d), mod_s3,
                          wa[l], wc[l], wh[l], wo[l], ln_m_g[l], ln_m_b[l], db, alpha)
        xs3 = _ffn_call(xs3, mod_s3, wfi[l], wfo[l], ln_f_g[l], ln_f_b[l], db, alpha)
        xs = xs3.reshape(db, d)
        ks_s.append(kn.reshape(db, 1, ATT_HEADS, ATT_HEAD_DIM))
        vs_s.append(vn.reshape(db, 1, ATT_HEADS, ATT_HEAD_DIM))
        convs_s.append(conv_s.reshape(db, 2, CONV_WIDTH))
        hgs_s.append(hg_s)

    return (xp, xs.reshape(db, 1, d), jnp.stack(ks), jnp.stack(vs), jnp.stack(convs), jnp.stack(hgs),
            jnp.stack(ks_s), jnp.stack(vs_s), jnp.stack(convs_s), jnp.stack(hgs_s))
```

```python
import functools
import math

import jax
import jax.numpy as jnp
from jax import lax
from jax.experimental import pallas as pl
from jax.experimental.pallas import tpu as pltpu

F32 = jnp.float32
BF16 = jnp.bfloat16

ATT_HEADS = 8
ATT_HEAD_DIM = 64
ATT_WIDTH = ATT_HEADS * ATT_HEAD_DIM
MOBA_BLOCK = 256
MOBA_TOPK = 3
CONV_WIDTH = 512
HG_HEADS = 4
HG_DK = 128
HG_DV = 128
HG_WIDTH = HG_HEADS * HG_DK
PAGE_SIZE = 128
LN_EPS = 1e-5
NEG_INF = -1e30
ATT_SCALE = ATT_HEAD_DIM ** -0.5
HG_SCALE = HG_DK ** -0.5
LOG2E = math.log2(math.e)

LANES = 128
HEAD_PAIR = LANES // ATT_HEAD_DIM
HG_CHUNK = 32
VMEM_LIMIT = 56 * 1024 * 1024


def _cparams(*sem):
    return pltpu.CompilerParams(dimension_semantics=sem, vmem_limit_bytes=VMEM_LIMIT)


def _mm(a, b):
    return jnp.dot(a, b, preferred_element_type=F32)


def _mm_nt(a, b):
    return lax.dot_general(a, b, (((1,), (1,)), ((), ())), preferred_element_type=F32)


def _mm_tn(a, b):
    return lax.dot_general(a, b, (((0,), (0,)), ((), ())), preferred_element_type=F32)


def _split2(x):
    hi = x.astype(BF16)
    lo = (x - hi.astype(F32)).astype(BF16)
    return hi, lo


def _mm_nt_precise(a, b):
    ah, al = _split2(a)
    bh, bl = _split2(b)
    return _mm_nt(ah, bh) + (_mm_nt(ah, bl) + _mm_nt(al, bh))


def _mm_precise(a, b):
    ah, al = _split2(a)
    bh, bl = _split2(b)
    return _mm(ah, bh) + (_mm(ah, bl) + _mm(al, bh))


def _sigmoid(x):
    return 1.0 / (1.0 + jnp.exp(-x))


def _silu(x):
    return x * _sigmoid(x)


def _layer_norm(x, g, b):
    mu = jnp.mean(x, axis=-1, keepdims=True)
    xc = x - mu
    var = jnp.mean(xc * xc, axis=-1, keepdims=True)
    return xc * lax.rsqrt(var + LN_EPS) * g + b


def _modulate(x, sc_ref, sh_ref):
    return (x * (1.0 + sc_ref[...]) + sh_ref[...]).astype(BF16)


def _cumsum_rows(x):
    n = x.shape[0]
    row = lax.broadcasted_iota(jnp.int32, x.shape, 0)
    d = 1
    while d < n:
        x = x + jnp.where(row >= d, pltpu.roll(x, d, 0), 0.0)
        d *= 2
    return x


def _const_spec(shape):
    nd = len(shape)
    return pl.BlockSpec(shape, lambda *_: (0,) * nd)


def _ln_kernel(x_ref, g_ref, b_ref, o_ref):
    o_ref[...] = _layer_norm(x_ref[...], g_ref[...], b_ref[...])


def _ln_call(x2, g, b, tm):
    n, d = x2.shape
    return pl.pallas_call(
        _ln_kernel,
        grid=(n // tm,),
        in_specs=[pl.BlockSpec((tm, d), lambda i: (i, 0)), _const_spec((1, d)), _const_spec((1, d))],
        out_specs=pl.BlockSpec((tm, d), lambda i: (i, 0)),
        out_shape=jax.ShapeDtypeStruct((n, d), F32),
        compiler_params=_cparams("parallel"),
        name="ln_in",
    )(x2, g.reshape(1, d), b.reshape(1, d))


def _ada_kernel(c_ref, w_ref, b_ref, o_ref):
    a = _silu(c_ref[...]).astype(BF16)
    o_ref[...] = _mm(a, w_ref[...].astype(BF16)) + b_ref[...]


def _ada_call(c_all, w_ada, b_ada):
    depth, d, n6 = w_ada.shape
    mp = c_all.shape[0]
    tn = n6 // 4
    return pl.pallas_call(
        _ada_kernel,
        grid=(depth, n6 // tn),
        in_specs=[pl.BlockSpec((mp, d), lambda l, j: (0, 0)),
                  pl.BlockSpec((None, d, tn), lambda l, j: (l, 0, j)),
                  pl.BlockSpec((None, 1, tn), lambda l, j: (l, 0, j))],
        out_specs=pl.BlockSpec((None, mp, tn), lambda l, j: (l, 0, j)),
        out_shape=jax.ShapeDtypeStruct((depth, mp, n6), F32),
        compiler_params=_cparams("parallel", "parallel"),
        name="ada_mod",
    )(c_all, w_ada, b_ada.reshape(depth, 1, n6))


def _mod_spec(rows, d, chunk):
    return pl.BlockSpec((None, rows, d), lambda b, t: (b, 0, chunk))


def _qkv_kernel(x_ref, sc_ref, sh_ref, w_ref, wvt_ref,
                q_ref, qf_ref, k_ref, v_ref, kb_ref, vt_ref, km_ref):
    tm = x_ref.shape[0]
    u = _modulate(x_ref[...], sc_ref, sh_ref)
    p = _mm(u, w_ref[...])
    q = p[:, :ATT_WIDTH]
    k = p[:, ATT_WIDTH:2 * ATT_WIDTH]
    v = p[:, 2 * ATT_WIDTH:]
    q_ref[...] = (q * (ATT_SCALE * LOG2E)).astype(BF16)
    qf_ref[...] = q
    k_ref[...] = k
    v_ref[...] = v
    kb_ref[...] = k.astype(BF16)
    vt = _mm_nt(wvt_ref[...], u).astype(BF16)
    for i in range(tm // MOBA_BLOCK):
        rows = slice(i * MOBA_BLOCK, (i + 1) * MOBA_BLOCK)
        vt_ref[i] = vt[:, rows]
        km_ref[i] = jnp.sum(k[rows], axis=0, keepdims=True) * (1.0 / MOBA_BLOCK)


def _qkv_call(x3, mod3, w_qkv, w_vt, tm):
    b, s, d = x3.shape
    nbt = tm // MOBA_BLOCK
    nb = s // MOBA_BLOCK
    row = lambda bb, t: (bb, t, 0)
    act = pl.BlockSpec((None, tm, ATT_WIDTH), row)
    return pl.pallas_call(
        _qkv_kernel,
        grid=(b, s // tm),
        in_specs=[pl.BlockSpec((None, tm, d), row), _mod_spec(1, d, 1), _mod_spec(1, d, 0),
                  _const_spec(w_qkv.shape), _const_spec(w_vt.shape)],
        out_specs=[act, act, act, act, act,
                   pl.BlockSpec((None, nbt, ATT_WIDTH, MOBA_BLOCK), lambda bb, t: (bb, t, 0, 0)),
                   pl.BlockSpec((None, nbt, 1, ATT_WIDTH), lambda bb, t: (bb, t, 0, 0))],
        out_shape=[jax.ShapeDtypeStruct((b, s, ATT_WIDTH), BF16),
                   jax.ShapeDtypeStruct((b, s, ATT_WIDTH), F32),
                   jax.ShapeDtypeStruct((b, s, ATT_WIDTH), F32),
                   jax.ShapeDtypeStruct((b, s, ATT_WIDTH), F32),
                   jax.ShapeDtypeStruct((b, s, ATT_WIDTH), BF16),
                   jax.ShapeDtypeStruct((b, nb, ATT_WIDTH, MOBA_BLOCK), BF16),
                   jax.ShapeDtypeStruct((b, nb, 1, ATT_WIDTH), F32)],
        compiler_params=_cparams("parallel", "parallel"),
        name="proj_qkv",
    )(x3, mod3, mod3, w_qkv, w_vt)


def _conv_kernel(x_ref, sc_ref, sh_ref, w_ref, cw_ref, cb_ref, y_ref, st_ref, carry_ref):
    tm = x_ref.shape[0]
    t = pl.program_id(1)

    @pl.when(t == 0)
    def _():
        carry_ref[...] = jnp.zeros_like(carry_ref)

    u = _modulate(x_ref[...], sc_ref, sh_ref)
    p = _mm(u, w_ref[...])
    gate_b = p[:, :CONV_WIDTH]
    ucv = p[:, CONV_WIDTH:2 * CONV_WIDTH] * p[:, 2 * CONV_WIDTH:]
    prev1 = carry_ref[7:8, :]
    prev2 = carry_ref[6:7, :]
    row = lax.broadcasted_iota(jnp.int32, ucv.shape, 0)
    s1 = jnp.where(row == 0, prev1, pltpu.roll(ucv, 1, 0))
    s2 = jnp.where(row == 0, prev2, jnp.where(row == 1, prev1, pltpu.roll(ucv, 2, 0)))
    cw = cw_ref[...]
    z = cb_ref[...] + s2 * cw[0:1, :] + s1 * cw[1:2, :] + ucv * cw[2:3, :]
    y_ref[...] = (gate_b * z).astype(BF16)
    carry_ref[...] = ucv[tm - 8:, :]

    @pl.when(t == pl.num_programs(1) - 1)
    def _():
        st_ref[...] = ucv[tm - 2:, :]


def _conv_call(x3, mod3, w_conv, conv_w, conv_b, tm):
    b, s, d = x3.shape
    row = lambda bb, t: (bb, t, 0)
    return pl.pallas_call(
        _conv_kernel,
        grid=(b, s // tm),
        in_specs=[pl.BlockSpec((None, tm, d), row), _mod_spec(1, d, 1), _mod_spec(1, d, 0),
                  _const_spec(w_conv.shape), _const_spec(conv_w.shape), _const_spec((1, CONV_WIDTH))],
        out_specs=[pl.BlockSpec((None, tm, CONV_WIDTH), row),
                   pl.BlockSpec((None, 2, CONV_WIDTH), lambda bb, t: (bb, 0, 0))],
        out_shape=[jax.ShapeDtypeStruct((b, s, CONV_WIDTH), BF16),
                   jax.ShapeDtypeStruct((b, 2, CONV_WIDTH), F32)],
        scratch_shapes=[pltpu.VMEM((8, CONV_WIDTH), F32)],
        compiler_params=_cparams("parallel", "arbitrary"),
        name="proj_conv",
    )(x3, mod3, mod3, w_conv, conv_w, conv_b.reshape(1, CONV_WIDTH))


def _hgproj_kernel(x_ref, sc_ref, sh_ref, w_ref, lb_ref, q_ref, lf_ref, v_ref, g_ref):
    u = _modulate(x_ref[...], sc_ref, sh_ref)
    p = _mm(u, w_ref[...])
    lb = lb_ref[...]
    q_ref[...] = p[:, :HG_WIDTH] * HG_SCALE
    f = lb + (1.0 - lb) * _sigmoid(p[:, HG_WIDTH:2 * HG_WIDTH])
    lf_ref[...] = jnp.log(f)
    v_ref[...] = p[:, 2 * HG_WIDTH:3 * HG_WIDTH].astype(BF16)
    g_ref[...] = p[:, 3 * HG_WIDTH:]


def _hgproj_call(x3, mod3, w_hg, lb, tm):
    b, s, d = x3.shape
    row = lambda bb, t: (bb, t, 0)
    act = pl.BlockSpec((None, tm, HG_WIDTH), row)
    return pl.pallas_call(
        _hgproj_kernel,
        grid=(b, s // tm),
        in_specs=[pl.BlockSpec((None, tm, d), row), _mod_spec(1, d, 1), _mod_spec(1, d, 0),
                  _const_spec(w_hg.shape), _const_spec((1, HG_WIDTH))],
        out_specs=[act, act, act, act],
        out_shape=[jax.ShapeDtypeStruct((b, s, HG_WIDTH), F32),
                   jax.ShapeDtypeStruct((b, s, HG_WIDTH), F32),
                   jax.ShapeDtypeStruct((b, s, HG_WIDTH), BF16),
                   jax.ShapeDtypeStruct((b, s, HG_WIDTH), F32)],
        compiler_params=_cparams("parallel", "parallel"),
        name="proj_hgrn",
    )(x3, mod3, mod3, w_hg, lb.reshape(1, HG_WIDTH))


def _gates_kernel(x_ref, sc_ref, sh_ref, w_ref, o_ref):
    u = _modulate(x_ref[...], sc_ref, sh_ref)
    o_ref[...] = _sigmoid(_mm(u, w_ref[...])).astype(BF16)


def _gates_call(x3, mod3, w_g, tm):
    b, s, d = x3.shape
    n = w_g.shape[1]
    row = lambda bb, t: (bb, t, 0)
    return pl.pallas_call(
        _gates_kernel,
        grid=(b, s // tm),
        in_specs=[pl.BlockSpec((None, tm, d), row), _mod_spec(1, d, 1), _mod_spec(1, d, 0),
                  _const_spec(w_g.shape)],
        out_specs=pl.BlockSpec((None, tm, n), row),
        out_shape=jax.ShapeDtypeStruct((b, s, n), BF16),
        compiler_params=_cparams("parallel", "parallel"),
        name="proj_gates",
    )(x3, mod3, mod3, w_g)


MOBA_KEY_CHUNK = 64


def _moba_block(j, rows, bias, kb_ref, vt_ref, qm_ref, s_ref, p_ref, ml_ref, acc_ref):
    nchunk = MOBA_BLOCK // MOBA_KEY_CHUNK
    sub = 8
    blk_max = []
    for h in range(ATT_HEADS):
        hp = h // HEAD_PAIR
        s = _mm_nt(kb_ref[j, :, hp * LANES:(hp + 1) * LANES], qm_ref[h])
        if bias is not None:
            s = s + bias
        s_ref[h] = s
        blk_max.append(jnp.max(s.reshape(MOBA_BLOCK // sub, sub, MOBA_BLOCK), axis=0))
    alphas = []
    for h in range(ATT_HEADS):
        m = ml_ref[h:h + 1, :]
        l = ml_ref[ATT_HEADS + h:ATT_HEADS + h + 1, :]
        m_new = jnp.maximum(m, jnp.max(blk_max[h], axis=0, keepdims=True) + rows[h])
        shift = jnp.where(rows[h] < 0.0, -NEG_INF, m_new)
        l_part = jnp.zeros((sub, MOBA_BLOCK), F32)
        for c in range(nchunk):
            keys = slice(c * MOBA_KEY_CHUNK, (c + 1) * MOBA_KEY_CHUNK)
            p = jnp.exp2(s_ref[h, keys, :] - shift)
            p_ref[h, keys, :] = p.astype(BF16)
            l_part = l_part + jnp.sum(p.reshape(MOBA_KEY_CHUNK // sub, sub, MOBA_BLOCK), axis=0)
        alpha = jnp.exp2(m - m_new)
        ml_ref[h:h + 1, :] = m_new
        ml_ref[ATT_HEADS + h:ATT_HEADS + h + 1, :] = alpha * l + jnp.sum(l_part, axis=0, keepdims=True)
        alphas.append(alpha)
    for h in range(ATT_HEADS):
        vt_h = vt_ref[j, h * ATT_HEAD_DIM:(h + 1) * ATT_HEAD_DIM, :]
        acc_ref[h] = alphas[h] * acc_ref[h] + _mm(vt_h, p_ref[h])


def _moba_kernel(q_ref, qf_ref, kb_ref, vt_ref, km_ref, o_ref,
                 qm_ref, sel_ref, ml_ref, acc_ref, s_ref, p_ref):
    nbp = km_ref.shape[0]
    blk = MOBA_BLOCK
    qi = pl.program_id(1)
    lane = lax.broadcasted_iota(jnp.int32, (blk, LANES), 1)
    blk_id = lax.broadcasted_iota(jnp.int32, (nbp, blk), 0)
    past = blk_id < qi
    for h in range(ATT_HEADS):
        hp, hh = divmod(h, HEAD_PAIR)
        cols = slice(hp * LANES, (hp + 1) * LANES)
        in_head = (lane >= hh * ATT_HEAD_DIM) & (lane < (hh + 1) * ATT_HEAD_DIM)
        q = q_ref[:, cols]
        qm_ref[h] = jnp.where(in_head, q, jnp.zeros_like(q))
        gate = _mm_nt_precise(km_ref[:, cols], jnp.where(in_head, qf_ref[:, cols], 0.0))
        gate = jnp.where(past, gate, NEG_INF)
        rank = jnp.zeros((nbp, blk), jnp.int32)
        for jp in range(nbp):
            gj = gate[jp:jp + 1, :]
            beats = (gj > gate) | ((gj == gate) & (blk_id > jp))
            rank = rank + beats.astype(jnp.int32)
        sel_ref[h] = jnp.where(past & (rank < MOBA_TOPK), 0.0, NEG_INF)
    ml_ref[0:ATT_HEADS, :] = jnp.full((ATT_HEADS, blk), NEG_INF, F32)
    ml_ref[ATT_HEADS:, :] = jnp.zeros((ATT_HEADS, blk), F32)
    acc_ref[...] = jnp.zeros_like(acc_ref)

    def body(j, carry):
        rows = [sel_ref[h, pl.ds(j, 1), :] for h in range(ATT_HEADS)]
        _moba_block(j, rows, None, kb_ref, vt_ref, qm_ref, s_ref, p_ref, ml_ref, acc_ref)
        return carry

    lax.fori_loop(0, qi, body, 0)

    key = lax.broadcasted_iota(jnp.int32, (blk, blk), 0)
    qry = lax.broadcasted_iota(jnp.int32, (blk, blk), 1)
    causal = jnp.where(key <= qry, 0.0, NEG_INF)
    zero_row = jnp.zeros((1, blk), F32)
    _moba_block(qi, [zero_row] * ATT_HEADS, causal, kb_ref, vt_ref, qm_ref, s_ref, p_ref, ml_ref, acc_ref)
    outs = [acc_ref[h] / ml_ref[ATT_HEADS + h:ATT_HEADS + h + 1, :] for h in range(ATT_HEADS)]
    o_ref[...] = jnp.concatenate(outs, axis=0).T.astype(BF16)


def _moba_call(q, qf, kb4, vt4, km3):
    b, s, _ = q.shape
    nb = s // MOBA_BLOCK
    nbp = km3.shape[1]
    qspec = pl.BlockSpec((None, MOBA_BLOCK, ATT_WIDTH), lambda bb, qi: (bb, qi, 0))
    return pl.pallas_call(
        _moba_kernel,
        grid=(b, nb),
        in_specs=[qspec, qspec,
                  pl.BlockSpec((None, nb, MOBA_BLOCK, ATT_WIDTH), lambda bb, qi: (bb, 0, 0, 0)),
                  pl.BlockSpec((None, nb, ATT_WIDTH, MOBA_BLOCK), lambda bb, qi: (bb, 0, 0, 0)),
                  pl.BlockSpec((None, nbp, ATT_WIDTH), lambda bb, qi: (bb, 0, 0))],
        out_specs=qspec,
        out_shape=jax.ShapeDtypeStruct((b, s, ATT_WIDTH), BF16),
        scratch_shapes=[pltpu.VMEM((ATT_HEADS, MOBA_BLOCK, LANES), BF16),
                        pltpu.VMEM((ATT_HEADS, nbp, MOBA_BLOCK), F32),
                        pltpu.VMEM((2 * ATT_HEADS, MOBA_BLOCK), F32),
                        pltpu.VMEM((ATT_HEADS, ATT_HEAD_DIM, MOBA_BLOCK), F32),
                        pltpu.VMEM((ATT_HEADS, MOBA_BLOCK, MOBA_BLOCK), F32),
                        pltpu.VMEM((ATT_HEADS, MOBA_BLOCK, MOBA_BLOCK), BF16)],
        compiler_params=_cparams("parallel", "arbitrary"),
        name="moba_prompt",
    )(q, qf, kb4, vt4, km3)


def _hgrn_kernel(q_ref, lf_ref, v_ref, g_ref, gn_ref, y_ref, so_ref, st_ref):
    th = q_ref.shape[0]
    c = HG_CHUNK
    t = pl.program_id(1)

    @pl.when(t == 0)
    def _():
        st_ref[...] = jnp.zeros_like(st_ref)

    tril = (lax.broadcasted_iota(jnp.int32, (c, c), 0) >= lax.broadcasted_iota(jnp.int32, (c, c), 1))
    gn = gn_ref[...]
    for ci in range(th // c):
        rows = slice(ci * c, (ci + 1) * c)
        for h in range(HG_HEADS):
            lanes = slice(h * HG_DK, (h + 1) * HG_DK)
            lf = lf_ref[rows, lanes]
            q = q_ref[rows, lanes]
            v = v_ref[rows, lanes]
            g = _cumsum_rows(lf)
            gmid = g[c // 2 - 1:c // 2, :]
            glast = g[c - 1:c, :]
            k = 1.0 - jnp.exp(lf)
            a = _mm_nt((q * jnp.exp(g - gmid)).astype(BF16), (k * jnp.exp(gmid - g)).astype(BF16))
            a = jnp.where(tril, a, 0.0)
            st = st_ref[h]
            o = (_mm_nt((q * jnp.exp(g)).astype(BF16), st.astype(BF16))
                 + _mm(a.astype(BF16), v))
            kd = (k * jnp.exp(glast - g)).astype(BF16)
            st_ref[h] = st * jnp.exp(glast) + _mm_tn(v, kd)
            rms = lax.rsqrt(jnp.mean(o * o, axis=-1, keepdims=True) + LN_EPS)
            y_ref[rows, lanes] = (o * rms * gn * _silu(g_ref[rows, lanes])).astype(BF16)

    @pl.when(t == pl.num_programs(1) - 1)
    def _():
        for h in range(HG_HEADS):
            so_ref[h] = st_ref[h].T


def _hgrn_call(q, lf, v, g, gn, th):
    b, s, _ = q.shape
    row = lambda bb, t: (bb, t, 0)
    act = pl.BlockSpec((None, th, HG_WIDTH), row)
    return pl.pallas_call(
        _hgrn_kernel,
        grid=(b, s // th),
        in_specs=[act, act, act, act, _const_spec((1, HG_DV))],
        out_specs=[act, pl.BlockSpec((None, HG_HEADS, HG_DK, HG_DV), lambda bb, t: (bb, 0, 0, 0))],
        out_shape=[jax.ShapeDtypeStruct((b, s, HG_WIDTH), BF16),
                   jax.ShapeDtypeStruct((b, HG_HEADS, HG_DK, HG_DV), F32)],
        scratch_shapes=[pltpu.VMEM((HG_HEADS, HG_DV, HG_DK), F32)],
        compiler_params=_cparams("parallel", "arbitrary"),
        name="hgrn_prompt",
    )(q, lf, v, g, gn.reshape(1, HG_DV))


def _merge_kernel(x_ref, ya_ref, yc_ref, yh_ref, gates_ref, gt_ref,
                  wa_ref, wc_ref, wh_ref, wo_ref, lng_ref, lnb_ref, o_ref, *, alpha):
    d = x_ref.shape[1]
    gates = gates_ref[...]
    merged = (gates[:, :d].astype(F32) * _mm(ya_ref[...], wa_ref[...])
              + gates[:, d:2 * d].astype(F32) * _mm(yc_ref[...], wc_ref[...])
              + gates[:, 2 * d:].astype(F32) * _mm(yh_ref[...], wh_ref[...]))
    y = _mm(merged.astype(BF16), wo_ref[...])
    o_ref[...] = _layer_norm(alpha * x_ref[...] + gt_ref[...] * y, lng_ref[...], lnb_ref[...])


def _merge_call(x3, ya, yc, yh, gates, mod3, wa, wc, wh, wo, lng, lnb, tm, alpha):
    b, s, d = x3.shape
    mod_rows = mod3.shape[1]
    mrows = 1 if mod_rows == 1 else tm
    row = lambda bb, t: (bb, t, 0)
    br = pl.BlockSpec((None, tm, ya.shape[2]), row)
    xs = pl.BlockSpec((None, tm, d), row)
    return pl.pallas_call(
        functools.partial(_merge_kernel, alpha=alpha),
        grid=(b, s // tm),
        in_specs=[xs, br, br, br, pl.BlockSpec((None, tm, 3 * d), row), _mod_spec(mrows, d, 2),
                  _const_spec(wa.shape), _const_spec(wc.shape), _const_spec(wh.shape),
                  _const_spec(wo.shape), _const_spec((1, d)), _const_spec((1, d))],
        out_specs=xs,
        out_shape=jax.ShapeDtypeStruct((b, s, d), F32),
        compiler_params=_cparams("parallel", "parallel"),
        name="merge",
    )(x3, ya, yc, yh, gates, mod3, wa, wc, wh, wo, lng.reshape(1, d), lnb.reshape(1, d))


def _ffn_kernel(x_ref, sc_ref, sh_ref, gt_ref, wi_ref, wo_ref, lng_ref, lnb_ref, o_ref, *, alpha):
    x = x_ref[...]
    dff = wo_ref.shape[0]
    h = _mm(_modulate(x, sc_ref, sh_ref), wi_ref[...])
    act = (_silu(h[:, :dff]) * h[:, dff:]).astype(BF16)
    y = _mm(act, wo_ref[...])
    o_ref[...] = _layer_norm(alpha * x + gt_ref[...] * y, lng_ref[...], lnb_ref[...])


def _ffn_call(x3, mod3, wi, wo, lng, lnb, tm, alpha):
    b, s, d = x3.shape
    mod_rows = mod3.shape[1]
    mrows = 1 if mod_rows == 1 else tm
    row = lambda bb, t: (bb, t, 0)
    xs = pl.BlockSpec((None, tm, d), row)
    return pl.pallas_call(
        functools.partial(_ffn_kernel, alpha=alpha),
        grid=(b, s // tm),
        in_specs=[xs, _mod_spec(mrows, d, 4), _mod_spec(mrows, d, 3), _mod_spec(mrows, d, 5),
                  _const_spec(wi.shape), _const_spec(wo.shape), _const_spec((1, d)), _const_spec((1, d))],
        out_specs=xs,
        out_shape=jax.ShapeDtypeStruct((b, s, d), F32),
        compiler_params=_cparams("parallel", "parallel"),
        name="ffn",
    )(x3, mod3, mod3, mod3, wi, wo, lng.reshape(1, d), lnb.reshape(1, d))


def _kmean_kernel(pt_ref, *refs):
    o_ref = refs[-1]
    pages = refs[:-1]
    ppb = MOBA_BLOCK // PAGE_SIZE
    bps = len(pages) // ppb
    c = pl.program_id(2)

    @pl.when(c == 0)
    def _():
        o_ref[...] = jnp.zeros_like(o_ref)

    lane = lax.broadcasted_iota(jnp.int32, o_ref.shape, 1)
    acc = o_ref[...]
    for i in range(bps):
        tot = pages[ppb * i][...]
        for r in range(1, ppb):
            tot = tot + pages[ppb * i + r][...]
        mean = jnp.sum(tot, axis=-1, keepdims=True) * (1.0 / MOBA_BLOCK)
        acc = jnp.where(lane == c * bps + i, mean, acc)
    o_ref[...] = acc


def _kmean_call(cache_kt, page_table, pps):
    depth = cache_kt.shape[0]
    db, n_pages = page_table.shape
    assert n_pages * PAGE_SIZE // MOBA_BLOCK <= LANES

    def page_spec(n):
        return pl.BlockSpec((None, None, ATT_WIDTH, PAGE_SIZE),
                            lambda l, b, c, pt: (l, pt[b, c * pps + n], 0, 0))

    return pl.pallas_call(
        _kmean_kernel,
        grid_spec=pltpu.PrefetchScalarGridSpec(
            num_scalar_prefetch=1,
            grid=(depth, db, n_pages // pps),
            in_specs=[page_spec(n) for n in range(pps)],
            out_specs=pl.BlockSpec((None, None, ATT_WIDTH, LANES), lambda l, b, c, pt: (l, b, 0, 0)),
        ),
        out_shape=jax.ShapeDtypeStruct((depth, db, ATT_WIDTH, LANES), F32),
        compiler_params=_cparams("parallel", "parallel", "arbitrary"),
        name="cache_kmean",
    )(page_table, *([cache_kt] * pps))


def _sproj_kernel(x_ref, sc_ref, sh_ref, w_ref, o_ref):
    o_ref[...] = _mm(_modulate(x_ref[...], sc_ref, sh_ref), w_ref[...])


def _sproj_call(x2, mod2, w_in, tn):
    m, d = x2.shape
    n = w_in.shape[1]
    return pl.pallas_call(
        _sproj_kernel,
        grid=(n // tn,),
        in_specs=[pl.BlockSpec((m, d), lambda j: (0, 0)),
                  pl.BlockSpec((m, d), lambda j: (0, 1)),
                  pl.BlockSpec((m, d), lambda j: (0, 0)),
                  pl.BlockSpec((d, tn), lambda j: (0, j))],
        out_specs=pl.BlockSpec((m, tn), lambda j: (0, j)),
        out_shape=jax.ShapeDtypeStruct((m, n), F32),
        compiler_params=_cparams("parallel"),
        name="sample_proj",
    )(x2, mod2, mod2, w_in)


def _ssel_kernel(q_ref, kmt_ref, o_ref, *, nbs):
    head = lax.broadcasted_iota(jnp.int32, (ATT_HEADS, ATT_WIDTH), 0)
    dim = lax.broadcasted_iota(jnp.int32, (ATT_HEADS, ATT_WIDTH), 1)
    q_heads = jnp.where(dim // ATT_HEAD_DIM == head, q_ref[...], 0.0)
    gate = _mm_precise(q_heads, kmt_ref[...])
    lane = lax.broadcasted_iota(jnp.int32, (ATT_HEADS, LANES), 1)
    gate = jnp.where(lane < nbs, gate, -jnp.inf)
    out = jnp.zeros((ATT_HEADS, LANES), jnp.int32)
    for r in range(MOBA_TOPK):
        best = jnp.max(gate, axis=-1, keepdims=True)
        idx = jnp.min(jnp.where(gate == best, lane, LANES), axis=-1, keepdims=True)
        out = jnp.where(lane == r, idx, out)
        gate = jnp.where(lane == idx, -jnp.inf, gate)
    o_ref[...] = out


def _ssel_call(q3, kmt_l, nbs):
    db = q3.shape[0]
    return pl.pallas_call(
        functools.partial(_ssel_kernel, nbs=nbs),
        grid=(db,),
        in_specs=[pl.BlockSpec((None, 1, ATT_WIDTH), lambda b: (b, 0, 0)),
                  pl.BlockSpec((None, ATT_WIDTH, LANES), lambda b: (b, 0, 0))],
        out_specs=pl.BlockSpec((None, ATT_HEADS, LANES), lambda b: (b, 0, 0)),
        out_shape=jax.ShapeDtypeStruct((db, ATT_HEADS, LANES), jnp.int32),
        compiler_params=_cparams("parallel"),
        name="sample_select",
    )(q3, kmt_l)


def _sattn_kernel(sel_ref, pt_ref, q_ref, kn_ref, vn_ref, *refs):
    o_ref = refs[-1]
    npg = (len(refs) - 1) // 2
    kpages = refs[:npg]
    vpages = refs[npg:2 * npg]
    per_head = npg // HEAD_PAIR
    lane = lax.broadcasted_iota(jnp.int32, (1, LANES), 1)
    q = q_ref[...] * ATT_SCALE
    kn = kn_ref[...]
    vn = vn_ref[...]
    out = jnp.zeros((1, LANES), F32)
    for h in range(HEAD_PAIR):
        in_head = (lane >= h * ATT_HEAD_DIM) & (lane < (h + 1) * ATT_HEAD_DIM)
        qh = jnp.where(in_head, q, 0.0)
        q8 = jnp.broadcast_to(qh, (8, LANES)).astype(BF16)
        s_self = jnp.sum(qh * kn, axis=-1, keepdims=True)
        scores = [_mm(q8, kpages[h * per_head + i][...].astype(BF16)) for i in range(per_head)]
        m = s_self
        for s in scores:
            m = jnp.maximum(m, jnp.max(s[0:1, :], axis=-1, keepdims=True))
        p_self = jnp.exp(s_self - m)
        l = p_self
        acc = p_self * vn
        for i, s in enumerate(scores):
            p = jnp.exp(s - m)
            l = l + jnp.sum(p[0:1, :], axis=-1, keepdims=True)
            acc = acc + _mm_nt(p.astype(BF16), vpages[h * per_head + i][...].astype(BF16))[0:1, :]
        out = jnp.where(in_head, acc / l, out)
    o_ref[...] = out.astype(BF16)


def _sattn_call(sel_flat, page_table, q3, kn3, vn3, cache_kt, cache_vt, layer):
    db = q3.shape[0]
    ppb = MOBA_BLOCK // PAGE_SIZE
    npair = ATT_HEADS // HEAD_PAIR
    vec = pl.BlockSpec((None, 1, LANES), lambda b, hp, sel, pt: (b, 0, hp))

    def page_spec(h, i, r):
        def imap(b, hp, sel, pt):
            blk = sel[(b * ATT_HEADS + hp * HEAD_PAIR + h) * MOBA_TOPK + i]
            return (layer, pt[b, blk * ppb + r], hp, 0)
        return pl.BlockSpec((None, None, LANES, PAGE_SIZE), imap)

    pages = [page_spec(h, i, r) for h in range(HEAD_PAIR) for i in range(MOBA_TOPK) for r in range(ppb)]
    return pl.pallas_call(
        _sattn_kernel,
        grid_spec=pltpu.PrefetchScalarGridSpec(
            num_scalar_prefetch=2,
            grid=(db, npair),
            in_specs=[vec, vec, vec] + pages + pages,
            out_specs=vec,
        ),
        out_shape=jax.ShapeDtypeStruct((db, 1, ATT_WIDTH), BF16),
        compiler_params=_cparams("parallel", "parallel"),
        name="sample_attn",
    )(sel_flat, page_table, q3, kn3, vn3, *([cache_kt] * len(pages)), *([cache_vt] * len(pages)))


def _sbranch_kernel(p_ref, cs_ref, hs_ref, cw_ref, cb_ref, lb_ref, gn_ref,
                    yc_ref, yh_ref, gates_ref, cso_ref, hso_ref):
    nb = p_ref.shape[0]
    p = p_ref[...]
    o0 = 3 * ATT_WIDTH
    gate_b = p[:, o0:o0 + CONV_WIDTH]
    ucv = p[:, o0 + CONV_WIDTH:o0 + 2 * CONV_WIDTH] * p[:, o0 + 2 * CONV_WIDTH:o0 + 3 * CONV_WIDTH]
    cs = cs_ref[...]
    prev2 = cs[:, :CONV_WIDTH]
    prev1 = cs[:, CONV_WIDTH:]
    cw = cw_ref[...]
    z = cb_ref[...] + prev2 * cw[0:1, :] + prev1 * cw[1:2, :] + ucv * cw[2:3, :]
    yc_ref[...] = (gate_b * z).astype(BF16)
    cso_ref[...] = jnp.concatenate([prev1, ucv], axis=1)

    o1 = o0 + 3 * CONV_WIDTH
    lb = lb_ref[...]
    hq = p[:, o1:o1 + HG_WIDTH] * HG_SCALE
    f = lb + (1.0 - lb) * _sigmoid(p[:, o1 + HG_WIDTH:o1 + 2 * HG_WIDTH])
    hv = p[:, o1 + 2 * HG_WIDTH:o1 + 3 * HG_WIDTH]
    hg = p[:, o1 + 3 * HG_WIDTH:o1 + 4 * HG_WIDTH]
    qt = hq.T
    ft = f.T
    kt = (1.0 - f).T
    gn = gn_ref[...]
    rows = []
    for b in range(nb):
        heads = []
        for h in range(HG_HEADS):
            ks = slice(h * HG_DK, (h + 1) * HG_DK)
            s_new = ft[ks, b:b + 1] * hs_ref[b, h] + kt[ks, b:b + 1] * hv[b:b + 1, ks]
            hso_ref[b, h] = s_new
            o = jnp.sum(qt[ks, b:b + 1] * s_new, axis=0, keepdims=True)
            rms = lax.rsqrt(jnp.mean(o * o, axis=-1, keepdims=True) + LN_EPS)
            heads.append(o * rms * gn)
        rows.append(jnp.concatenate(heads, axis=1))
    yh_ref[...] = (jnp.concatenate(rows, axis=0) * _silu(hg)).astype(BF16)
    gates_ref[...] = _sigmoid(p[:, o1 + 4 * HG_WIDTH:]).astype(BF16)


def _sbranch_call(proj, conv_state2, hg_state, conv_w, conv_b, lb, gn, nb):
    db, n_in = proj.shape
    d3 = n_in - 3 * ATT_WIDTH - 3 * CONV_WIDTH - 4 * HG_WIDTH
    row = lambda g: (g, 0)
    st = pl.BlockSpec((nb, HG_HEADS, HG_DK, HG_DV), lambda g: (g, 0, 0, 0))
    return pl.pallas_call(
        _sbranch_kernel,
        grid=(db // nb,),
        in_specs=[pl.BlockSpec((nb, n_in), row), pl.BlockSpec((nb, 2 * CONV_WIDTH), row), st,
                  _const_spec(conv_w.shape), _const_spec((1, CONV_WIDTH)), _const_spec((1, HG_WIDTH)),
                  _const_spec((1, HG_DV))],
        out_specs=[pl.BlockSpec((nb, CONV_WIDTH), row), pl.BlockSpec((nb, HG_WIDTH), row),
                   pl.BlockSpec((nb, d3), row), pl.BlockSpec((nb, 2 * CONV_WIDTH), row), st],
        out_shape=[jax.ShapeDtypeStruct((db, CONV_WIDTH), BF16),
                   jax.ShapeDtypeStruct((db, HG_WIDTH), BF16),
                   jax.ShapeDtypeStruct((db, d3), BF16),
                   jax.ShapeDtypeStruct((db, 2 * CONV_WIDTH), F32),
                   jax.ShapeDtypeStruct(hg_state.shape, F32)],
        compiler_params=_cparams("parallel"),
        name="sample_branch",
    )(proj, conv_state2, hg_state, conv_w, conv_b.reshape(1, CONV_WIDTH), lb.reshape(1, HG_WIDTH),
      gn.reshape(1, HG_DV))


def kernel(x_prompt, x_sample, cache_k, cache_v, state_conv, state_hgrn, page_table, c_prompt, c_sample,
           ln_in_g, ln_in_b, w_ada, b_ada, w_mix_in, conv_w, conv_b, hg_lb_logits, hg_norm_g,
           w_br_attn, w_br_conv, w_br_hgrn, w_mix_out, ln_m_g, ln_m_b, w_ffn_in, w_ffn_out, ln_f_g, ln_f_b):
    bp, seq, d = x_prompt.shape
    db = x_sample.shape[0]
    depth = w_ada.shape[0]
    n_pages = page_table.shape[1]
    past_blocks = n_pages * PAGE_SIZE // MOBA_BLOCK
    assert x_sample.shape[1] == 1 and seq % MOBA_BLOCK == 0
    assert past_blocks >= MOBA_TOPK and (n_pages * PAGE_SIZE) % MOBA_BLOCK == 0
    alpha = (2 * depth) ** 0.25
    nb = seq // MOBA_BLOCK
    nbp = -(-nb // 8) * 8

    tm = min(512, seq)
    tm_ffn = min(256, seq)
    th = min(128, seq)

    o_conv = 3 * ATT_WIDTH
    o_hg = o_conv + 3 * CONV_WIDTH
    o_gate = o_hg + 4 * HG_WIDTH
    w_in = w_mix_in.astype(BF16)
    w_qkv = w_in[:, :, :o_conv]
    w_vt = jnp.swapaxes(w_in[:, :, 2 * ATT_WIDTH:o_conv], 1, 2)
    w_conv = w_in[:, :, o_conv:o_hg]
    w_hg = w_in[:, :, o_hg:o_gate]
    w_gate = w_in[:, :, o_gate:]
    wa, wc, wh, wo = (w.astype(BF16) for w in (w_br_attn, w_br_conv, w_br_hgrn, w_mix_out))
    wfi, wfo = w_ffn_in.astype(BF16), w_ffn_out.astype(BF16)
    lb_cum = jnp.cumsum(jax.nn.softmax(hg_lb_logits.astype(F32), axis=0), axis=0)
    lb_all = lb_cum - lb_cum[0]

    n_c = bp + db
    mp = -(-n_c // 8) * 8
    c_all = jnp.concatenate([c_prompt, c_sample, jnp.zeros((mp - n_c, d), F32)], axis=0)
    mod = _ada_call(c_all, w_ada, b_ada)

    cache_kt = jnp.transpose(cache_k, (0, 1, 3, 4, 2)).reshape(depth, cache_k.shape[1], ATT_WIDTH, PAGE_SIZE)
    cache_vt = jnp.transpose(cache_v, (0, 1, 3, 4, 2)).reshape(depth, cache_v.shape[1], ATT_WIDTH, PAGE_SIZE)
    kmt_s = _kmean_call(cache_kt, page_table, min(16, n_pages))

    xp = _ln_call(x_prompt.reshape(bp * seq, d), ln_in_g, ln_in_b, tm).reshape(bp, seq, d)
    xs = _ln_call(x_sample.reshape(db, d), ln_in_g, ln_in_b, db)

    ks, vs, convs, hgs = [], [], [], []
    ks_s, vs_s, convs_s, hgs_s = [], [], [], []
    conv_state2 = state_conv.reshape(depth, db, 2 * CONV_WIDTH)
    for l in range(depth):
        mod_p = mod[l, :bp].reshape(bp, 1, 6 * d)
        mod_s = mod[l, bp:bp + db]

        q, qf, k, v, kb, vt, km = _qkv_call(xp, mod_p, w_qkv[l], w_vt[l], tm)
        km3 = km.reshape(bp, nb, ATT_WIDTH)
        if nbp != nb:
            km3 = jnp.pad(km3, ((0, 0), (0, nbp - nb), (0, 0)))
        ya = _moba_call(q, qf, kb.reshape(bp, nb, MOBA_BLOCK, ATT_WIDTH), vt, km3)
        yc, conv_new = _conv_call(xp, mod_p, w_conv[l], conv_w[l], conv_b[l], tm)
        hq, hlf, hv, hg = _hgproj_call(xp, mod_p, w_hg[l], lb_all[l], tm)
        yh, hg_new = _hgrn_call(hq, hlf, hv, hg, hg_norm_g[l], th)
        gates = _gates_call(xp, mod_p, w_gate[l], tm)
        xp = _merge_call(xp, ya, yc, yh, gates, mod_p, wa[l], wc[l], wh[l], wo[l],
                         ln_m_g[l], ln_m_b[l], tm, alpha)
        xp = _ffn_call(xp, mod_p, wfi[l], wfo[l], ln_f_g[l], ln_f_b[l], tm_ffn, alpha)
        ks.append(k.reshape(bp, seq, ATT_HEADS, ATT_HEAD_DIM))
        vs.append(v.reshape(bp, seq, ATT_HEADS, ATT_HEAD_DIM))
        convs.append(conv_new)
        hgs.append(hg_new)

        proj = _sproj_call(xs, mod_s, w_in[l], 1024)
        q3 = proj[:, :ATT_WIDTH].reshape(db, 1, ATT_WIDTH)
        kn = proj[:, ATT_WIDTH:2 * ATT_WIDTH]
        vn = proj[:, 2 * ATT_WIDTH:o_conv]
        sel = _ssel_call(q3, kmt_s[l], past_blocks)
        sel_flat = sel[:, :, :MOBA_TOPK].reshape(-1)
        ya_s = _sattn_call(sel_flat, page_table, q3, kn.reshape(db, 1, ATT_WIDTH),
                           vn.reshape(db, 1, ATT_WIDTH), cache_kt, cache_vt, l)
        yc_s, yh_s, gates_s, conv_s, hg_s = _sbranch_call(
            proj, conv_state2[l], state_hgrn[l], conv_w[l], conv_b[l], lb_all[l], hg_norm_g[l], min(8, db))
        xs3 = xs.reshape(1, db, d)
        mod_s3 = mod_s.reshape(1, db, 6 * d)
        xs3 = _merge_call(xs3, ya_s.reshape(1, db, ATT_WIDTH), yc_s.reshape(1, db, CONV_WIDTH),
                          yh_s.reshape(1, db, HG_WIDTH), gates_s.reshape(1, db, 3 * d), mod_s3,
                          wa[l], wc[l], wh[l], wo[l], ln_m_g[l], ln_m_b[l], db, alpha)
        xs3 = _ffn_call(xs3, mod_s3, wfi[l], wfo[l], ln_f_g[l], ln_f_b[l], db, alpha)
        xs = xs3.reshape(db, d)
        ks_s.append(kn.reshape(db, 1, ATT_HEADS, ATT_HEAD_DIM))
        vs_s.append(vn.reshape(db, 1, ATT_HEADS, ATT_HEAD_DIM))
        convs_s.append(conv_s.reshape(db, 2, CONV_WIDTH))
        hgs_s.append(hg_s)

    return (xp, xs.reshape(db, 1, d), jnp.stack(ks), jnp.stack(vs), jnp.stack(convs), jnp.stack(hgs),
            jnp.stack(ks_s), jnp.stack(vs_s), jnp.stack(convs_s), jnp.stack(hgs_s))
```

```python
import functools
import math

import jax
import jax.numpy as jnp
from jax import lax
from jax.experimental import pallas as pl
from jax.experimental.pallas import tpu as pltpu

F32 = jnp.float32
BF16 = jnp.bfloat16

ATT_HEADS = 8
ATT_HEAD_DIM = 64
ATT_WIDTH = ATT_HEADS * ATT_HEAD_DIM
MOBA_BLOCK = 256
MOBA_TOPK = 3
CONV_WIDTH = 512
HG_HEADS = 4
HG_DK = 128
HG_DV = 128
HG_WIDTH = HG_HEADS * HG_DK
PAGE_SIZE = 128
LN_EPS = 1e-5
NEG_INF = -1e30
ATT_SCALE = ATT_HEAD_DIM ** -0.5
HG_SCALE = HG_DK ** -0.5
LOG2E = math.log2(math.e)

LANES = 128
HEAD_PAIR = LANES // ATT_HEAD_DIM
HG_CHUNK = 32
VMEM_LIMIT = 56 * 1024 * 1024


def _cparams(*sem):
    return pltpu.CompilerParams(dimension_semantics=sem, vmem_limit_bytes=VMEM_LIMIT)


def _mm(a, b):
    return jnp.dot(a, b, preferred_element_type=F32)


def _mm_nt(a, b):
    return lax.dot_general(a, b, (((1,), (1,)), ((), ())), preferred_element_type=F32)


def _mm_tn(a, b):
    return lax.dot_general(a, b, (((0,), (0,)), ((), ())), preferred_element_type=F32)


def _split2(x):
    hi = x.astype(BF16)
    lo = (x - hi.astype(F32)).astype(BF16)
    return hi, lo


def _mm_nt_precise(a, b):
    ah, al = _split2(a)
    bh, bl = _split2(b)
    return _mm_nt(ah, bh) + (_mm_nt(ah, bl) + _mm_nt(al, bh))


def _mm_precise(a, b):
    ah, al = _split2(a)
    bh, bl = _split2(b)
    return _mm(ah, bh) + (_mm(ah, bl) + _mm(al, bh))


def _sigmoid(x):
    return 1.0 / (1.0 + jnp.exp(-x))


def _silu(x):
    return x * _sigmoid(x)


def _layer_norm(x, g, b):
    mu = jnp.mean(x, axis=-1, keepdims=True)
    xc = x - mu
    var = jnp.mean(xc * xc, axis=-1, keepdims=True)
    return xc * lax.rsqrt(var + LN_EPS) * g + b


def _modulate(x, sc_ref, sh_ref):
    return (x * (1.0 + sc_ref[...]) + sh_ref[...]).astype(BF16)


def _cumsum_rows(x):
    n = x.shape[0]
    row = lax.broadcasted_iota(jnp.int32, x.shape, 0)
    d = 1
    while d < n:
        x = x + jnp.where(row >= d, pltpu.roll(x, d, 0), 0.0)
        d *= 2
    return x


def _const_spec(shape):
    nd = len(shape)
    return pl.BlockSpec(shape, lambda *_: (0,) * nd)


def _ln_kernel(x_ref, g_ref, b_ref, o_ref):
    o_ref[...] = _layer_norm(x_ref[...], g_ref[...], b_ref[...])


def _ln_call(x2, g, b, tm):
    n, d = x2.shape
    return pl.pallas_call(
        _ln_kernel,
        grid=(n // tm,),
        in_specs=[pl.BlockSpec((tm, d), lambda i: (i, 0)), _const_spec((1, d)), _const_spec((1, d))],
        out_specs=pl.BlockSpec((tm, d), lambda i: (i, 0)),
        out_shape=jax.ShapeDtypeStruct((n, d), F32),
        compiler_params=_cparams("parallel"),
        name="ln_in",
    )(x2, g.reshape(1, d), b.reshape(1, d))


def _ada_kernel(c_ref, w_ref, b_ref, o_ref):
    a = _silu(c_ref[...]).astype(BF16)
    o_ref[...] = _mm(a, w_ref[...].astype(BF16)) + b_ref[...]


def _ada_call(c_all, w_ada, b_ada):
    depth, d, n6 = w_ada.shape
    mp = c_all.shape[0]
    tn = n6 // 4
    return pl.pallas_call(
        _ada_kernel,
        grid=(depth, n6 // tn),
        in_specs=[pl.BlockSpec((mp, d), lambda l, j: (0, 0)),
                  pl.BlockSpec((None, d, tn), lambda l, j: (l, 0, j)),
                  pl.BlockSpec((None, 1, tn), lambda l, j: (l, 0, j))],
        out_specs=pl.BlockSpec((None, mp, tn), lambda l, j: (l, 0, j)),
        out_shape=jax.ShapeDtypeStruct((depth, mp, n6), F32),
        compiler_params=_cparams("parallel", "parallel"),
        name="ada_mod",
    )(c_all, w_ada, b_ada.reshape(depth, 1, n6))


def _mod_spec(rows, d, chunk):
    return pl.BlockSpec((None, rows, d), lambda b, t: (b, 0, chunk))


O_CONV = 3 * ATT_WIDTH
O_HG = O_CONV + 3 * CONV_WIDTH
O_GATE = O_HG + 4 * HG_WIDTH


def _proj_kernel(x_ref, sc_ref, sh_ref, w_ref, wvt_ref, cw_ref, cb_ref, lb_ref,
                 q_ref, qf_ref, k_ref, v_ref, kb_ref, vt_ref, km_ref,
                 yc_ref, cst_ref, hq_ref, lf_ref, hv_ref, hg_ref, gates_ref, carry_ref):
    tm = x_ref.shape[0]
    t = pl.program_id(1)

    @pl.when(t == 0)
    def _():
        carry_ref[...] = jnp.zeros_like(carry_ref)

    u = _modulate(x_ref[...], sc_ref, sh_ref)

    p = _mm(u, w_ref[:, :O_CONV])
    q = p[:, :ATT_WIDTH]
    k = p[:, ATT_WIDTH:2 * ATT_WIDTH]
    q_ref[...] = (q * (ATT_SCALE * LOG2E)).astype(BF16)
    qf_ref[...] = q
    k_ref[...] = k
    v_ref[...] = p[:, 2 * ATT_WIDTH:]
    kb_ref[...] = k.astype(BF16)
    vt = _mm_nt(wvt_ref[...], u).astype(BF16)
    for i in range(tm // MOBA_BLOCK):
        rows = slice(i * MOBA_BLOCK, (i + 1) * MOBA_BLOCK)
        vt_ref[i] = vt[:, rows]
        km_ref[i] = jnp.sum(k[rows], axis=0, keepdims=True) * (1.0 / MOBA_BLOCK)

    p = _mm(u, w_ref[:, O_CONV:O_HG])
    ucv = p[:, CONV_WIDTH:2 * CONV_WIDTH] * p[:, 2 * CONV_WIDTH:]
    prev1 = carry_ref[7:8, :]
    prev2 = carry_ref[6:7, :]
    row = lax.broadcasted_iota(jnp.int32, ucv.shape, 0)
    s1 = jnp.where(row == 0, prev1, pltpu.roll(ucv, 1, 0))
    s2 = jnp.where(row == 0, prev2, jnp.where(row == 1, prev1, pltpu.roll(ucv, 2, 0)))
    cw = cw_ref[...]
    z = cb_ref[...] + s2 * cw[0:1, :] + s1 * cw[1:2, :] + ucv * cw[2:3, :]
    yc_ref[...] = (p[:, :CONV_WIDTH] * z).astype(BF16)
    carry_ref[...] = ucv[tm - 8:, :]

    @pl.when(t == pl.num_programs(1) - 1)
    def _():
        cst_ref[...] = ucv[tm - 2:, :]

    p = _mm(u, w_ref[:, O_HG:O_GATE])
    lb = lb_ref[...]
    hq_ref[...] = p[:, :HG_WIDTH] * HG_SCALE
    lf_ref[...] = jnp.log(lb + (1.0 - lb) * _sigmoid(p[:, HG_WIDTH:2 * HG_WIDTH]))
    hv_ref[...] = p[:, 2 * HG_WIDTH:3 * HG_WIDTH].astype(BF16)
    hg_ref[...] = p[:, 3 * HG_WIDTH:]

    gates_ref[...] = _sigmoid(_mm(u, w_ref[:, O_GATE:])).astype(BF16)


def _proj_call(x3, mod3, w_in, w_vt, conv_w, conv_b, lb, tm):
    b, s, d = x3.shape
    n_gate = w_in.shape[1] - O_GATE
    nbt = tm // MOBA_BLOCK
    nb = s // MOBA_BLOCK
    row = lambda bb, t: (bb, t, 0)
    blk4 = lambda bb, t: (bb, t, 0, 0)

    def rows(width):
        return pl.BlockSpec((None, tm, width), row)

    def arr(width, dtype):
        return jax.ShapeDtypeStruct((b, s, width), dtype)

    return pl.pallas_call(
        _proj_kernel,
        grid=(b, s // tm),
        in_specs=[rows(d), _mod_spec(1, d, 1), _mod_spec(1, d, 0),
                  pl.BlockSpec(w_in.shape, lambda bb, t: (0, 0), pipeline_mode=pl.Buffered(1)),
                  pl.BlockSpec(w_vt.shape, lambda bb, t: (0, 0), pipeline_mode=pl.Buffered(1)),
                  _const_spec(conv_w.shape), _const_spec((1, CONV_WIDTH)), _const_spec((1, HG_WIDTH))],
        out_specs=[rows(ATT_WIDTH)] * 5
                  + [pl.BlockSpec((None, nbt, ATT_WIDTH, MOBA_BLOCK), blk4),
                     pl.BlockSpec((None, nbt, 1, ATT_WIDTH), blk4),
                     rows(CONV_WIDTH), pl.BlockSpec((None, 2, CONV_WIDTH), lambda bb, t: (bb, 0, 0))]
                  + [rows(HG_WIDTH)] * 4 + [rows(n_gate)],
        out_shape=[arr(ATT_WIDTH, BF16), arr(ATT_WIDTH, F32), arr(ATT_WIDTH, F32), arr(ATT_WIDTH, F32),
                   arr(ATT_WIDTH, BF16),
                   jax.ShapeDtypeStruct((b, nb, ATT_WIDTH, MOBA_BLOCK), BF16),
                   jax.ShapeDtypeStruct((b, nb, 1, ATT_WIDTH), F32),
                   arr(CONV_WIDTH, BF16), jax.ShapeDtypeStruct((b, 2, CONV_WIDTH), F32),
                   arr(HG_WIDTH, F32), arr(HG_WIDTH, F32), arr(HG_WIDTH, BF16), arr(HG_WIDTH, F32),
                   arr(n_gate, BF16)],
        scratch_shapes=[pltpu.VMEM((8, CONV_WIDTH), F32)],
        compiler_params=_cparams("parallel", "arbitrary"),
        name="proj_in",
    )(x3, mod3, mod3, w_in, w_vt, conv_w, conv_b.reshape(1, CONV_WIDTH), lb.reshape(1, HG_WIDTH))


MOBA_KEY_CHUNK = 64


SUBLANES = 8


def _moba_scores(j, kb_ref, qm_ref, s_ref, bm_ref):
    for h in range(ATT_HEADS):
        hp = h // HEAD_PAIR
        s = _mm_nt(kb_ref[j, :, hp * LANES:(hp + 1) * LANES], qm_ref[h])
        s_ref[h] = s
        bm_ref[h] = jnp.max(s.reshape(MOBA_BLOCK // SUBLANES, SUBLANES, MOBA_BLOCK), axis=0)


def _moba_softmax(rows, bias, s_ref, bm_ref, p_ref, ml_ref):
    nchunk = MOBA_BLOCK // MOBA_KEY_CHUNK
    chunks = [slice(c * MOBA_KEY_CHUNK, (c + 1) * MOBA_KEY_CHUNK) for c in range(nchunk)]
    alphas = []
    for h in range(ATT_HEADS):
        m = ml_ref[h:h + 1, :]
        if bias is None:
            blk_max = jnp.max(bm_ref[h], axis=0, keepdims=True)
        else:
            blk_max = jnp.max(s_ref[h, chunks[0], :] + bias[chunks[0]], axis=0, keepdims=True)
            for c in chunks[1:]:
                blk_max = jnp.maximum(blk_max, jnp.max(s_ref[h, c, :] + bias[c], axis=0, keepdims=True))
        m_new = jnp.maximum(m, blk_max + rows[h])
        shift = jnp.where(rows[h] < 0.0, -NEG_INF, m_new)
        for c in chunks:
            s = s_ref[h, c, :]
            if bias is not None:
                s = s + bias[c]
            p_ref[h, c, :] = jnp.exp2(s - shift).astype(BF16)
        ml_ref[h:h + 1, :] = m_new
        alphas.append(jnp.exp2(m - m_new))
    return alphas


MOBA_ACC_ROWS = ATT_HEAD_DIM + 16


def _moba_values(j, alphas, vt_ref, p_ref, acc_ref):
    ones = jnp.ones((MOBA_ACC_ROWS - ATT_HEAD_DIM, MOBA_BLOCK), BF16)
    for h in range(ATT_HEADS):
        vt_h = jnp.concatenate([vt_ref[j, h * ATT_HEAD_DIM:(h + 1) * ATT_HEAD_DIM, :], ones], axis=0)
        acc_ref[h] = alphas[h] * acc_ref[h] + _mm(vt_h, p_ref[h])


def _moba_kernel(q_ref, qf_ref, kb_ref, vt_ref, km_ref, o_ref,
                 qm_ref, sel_ref, ml_ref, acc_ref, s_ref, bm_ref, p_ref):
    nbp = km_ref.shape[0]
    blk = MOBA_BLOCK
    qi = pl.program_id(1)
    lane = lax.broadcasted_iota(jnp.int32, (blk, LANES), 1)
    blk_id = lax.broadcasted_iota(jnp.int32, (nbp, blk), 0)
    past = blk_id < qi
    for h in range(ATT_HEADS):
        hp, hh = divmod(h, HEAD_PAIR)
        cols = slice(hp * LANES, (hp + 1) * LANES)
        in_head = (lane >= hh * ATT_HEAD_DIM) & (lane < (hh + 1) * ATT_HEAD_DIM)
        q = q_ref[:, cols]
        qm_ref[h] = jnp.where(in_head, q, jnp.zeros_like(q))
        gate = _mm_nt_precise(km_ref[:, cols], jnp.where(in_head, qf_ref[:, cols], 0.0))
        gate = jnp.where(past, gate, NEG_INF)
        rank = jnp.zeros((nbp, blk), jnp.int32)
        for jp in range(nbp):
            gj = gate[jp:jp + 1, :]
            beats = (gj > gate) | ((gj == gate) & (blk_id > jp))
            rank = rank + beats.astype(jnp.int32)
        sel_ref[h] = jnp.where(past & (rank < MOBA_TOPK), 0.0, NEG_INF)
    ml_ref[...] = jnp.full((ATT_HEADS, blk), NEG_INF, F32)
    acc_ref[...] = jnp.zeros_like(acc_ref)

    _moba_scores(0, kb_ref, qm_ref, s_ref, bm_ref)

    def body(j, carry):
        rows = [sel_ref[h, pl.ds(j, 1), :] for h in range(ATT_HEADS)]
        alphas = _moba_softmax(rows, None, s_ref, bm_ref, p_ref, ml_ref)
        _moba_scores(j + 1, kb_ref, qm_ref, s_ref, bm_ref)
        _moba_values(j, alphas, vt_ref, p_ref, acc_ref)
        return carry

    lax.fori_loop(0, qi, body, 0)

    key = lax.broadcasted_iota(jnp.int32, (blk, blk), 0)
    qry = lax.broadcasted_iota(jnp.int32, (blk, blk), 1)
    causal = jnp.where(key <= qry, 0.0, NEG_INF)
    zero_row = jnp.zeros((1, blk), F32)
    alphas = _moba_softmax([zero_row] * ATT_HEADS, causal, s_ref, bm_ref, p_ref, ml_ref)
    _moba_values(qi, alphas, vt_ref, p_ref, acc_ref)
    outs = [acc_ref[h, :ATT_HEAD_DIM, :] / acc_ref[h, ATT_HEAD_DIM:ATT_HEAD_DIM + 1, :]
            for h in range(ATT_HEADS)]
    o_ref[...] = jnp.concatenate(outs, axis=0).T.astype(BF16)


def _moba_call(q, qf, kb4, vt4, km3):
    b, s, _ = q.shape
    nb = s // MOBA_BLOCK
    nbp = km3.shape[1]
    qspec = pl.BlockSpec((None, MOBA_BLOCK, ATT_WIDTH), lambda bb, qi: (bb, qi, 0))
    return pl.pallas_call(
        _moba_kernel,
        grid=(b, nb),
        in_specs=[qspec, qspec,
                  pl.BlockSpec((None, nb, MOBA_BLOCK, ATT_WIDTH), lambda bb, qi: (bb, 0, 0, 0)),
                  pl.BlockSpec((None, nb, ATT_WIDTH, MOBA_BLOCK), lambda bb, qi: (bb, 0, 0, 0)),
                  pl.BlockSpec((None, nbp, ATT_WIDTH), lambda bb, qi: (bb, 0, 0))],
        out_specs=qspec,
        out_shape=jax.ShapeDtypeStruct((b, s, ATT_WIDTH), BF16),
        scratch_shapes=[pltpu.VMEM((ATT_HEADS, MOBA_BLOCK, LANES), BF16),
                        pltpu.VMEM((ATT_HEADS, nbp, MOBA_BLOCK), F32),
                        pltpu.VMEM((ATT_HEADS, MOBA_BLOCK), F32),
                        pltpu.VMEM((ATT_HEADS, MOBA_ACC_ROWS, MOBA_BLOCK), F32),
                        pltpu.VMEM((ATT_HEADS, MOBA_BLOCK, MOBA_BLOCK), F32),
                        pltpu.VMEM((ATT_HEADS, SUBLANES, MOBA_BLOCK), F32),
                        pltpu.VMEM((ATT_HEADS, MOBA_BLOCK, MOBA_BLOCK), BF16)],
        compiler_params=_cparams("parallel", "arbitrary"),
        name="moba_prompt",
    )(q, qf, kb4, vt4, km3)


def _hgrn_kernel(q_ref, lf_ref, v_ref, g_ref, gn_ref, y_ref, so_ref,
                 st_ref, qg_ref, oi_ref, ds_ref, dec_ref, a_ref):
    th = q_ref.shape[0]
    c = HG_CHUNK
    t = pl.program_id(1)

    @pl.when(t == 0)
    def _():
        st_ref[...] = jnp.zeros_like(st_ref)

    tril = (lax.broadcasted_iota(jnp.int32, (c, c), 0) >= lax.broadcasted_iota(jnp.int32, (c, c), 1))
    for ci in range(th // c):
        rows = slice(ci * c, (ci + 1) * c)
        for h in range(HG_HEADS):
            lanes = slice(h * HG_DK, (h + 1) * HG_DK)
            lf = lf_ref[rows, lanes]
            q = q_ref[rows, lanes]
            v = v_ref[rows, lanes]
            g = _cumsum_rows(lf)
            gmid = g[c // 2 - 1:c // 2, :]
            glast = g[c - 1:c, :]
            k = 1.0 - jnp.exp(lf)
            a = _mm_nt((q * jnp.exp(g - gmid)).astype(BF16), (k * jnp.exp(gmid - g)).astype(BF16))
            a_ref[ci, h] = jnp.where(tril, a, 0.0).astype(BF16)
            qg_ref[rows, lanes] = (q * jnp.exp(g)).astype(BF16)
            kd = (k * jnp.exp(glast - g)).astype(BF16)
            ds_ref[ci, h] = _mm_tn(v, kd)
            dec_ref[ci:ci + 1, lanes] = jnp.exp(glast)

    for ci in range(th // c):
        rows = slice(ci * c, (ci + 1) * c)
        for h in range(HG_HEADS):
            lanes = slice(h * HG_DK, (h + 1) * HG_DK)
            oi_ref[rows, lanes] = _mm(a_ref[ci, h], v_ref[rows, lanes])

    gn = gn_ref[...]
    for h in range(HG_HEADS):
        lanes = slice(h * HG_DK, (h + 1) * HG_DK)
        st = st_ref[h]
        for ci in range(th // c):
            rows = slice(ci * c, (ci + 1) * c)
            o = _mm_nt(qg_ref[rows, lanes], st.astype(BF16)) + oi_ref[rows, lanes]
            st = st * dec_ref[ci:ci + 1, lanes] + ds_ref[ci, h]
            rms = lax.rsqrt(jnp.mean(o * o, axis=-1, keepdims=True) + LN_EPS)
            y_ref[rows, lanes] = (o * rms * gn * _silu(g_ref[rows, lanes])).astype(BF16)
        st_ref[h] = st

    @pl.when(t == pl.num_programs(1) - 1)
    def _():
        for h in range(HG_HEADS):
            so_ref[h] = st_ref[h].T


def _hgrn_call(q, lf, v, g, gn, th):
    b, s, _ = q.shape
    row = lambda bb, t: (bb, t, 0)
    act = pl.BlockSpec((None, th, HG_WIDTH), row)
    return pl.pallas_call(
        _hgrn_kernel,
        grid=(b, s // th),
        in_specs=[act, act, act, act, _const_spec((1, HG_DV))],
        out_specs=[act, pl.BlockSpec((None, HG_HEADS, HG_DK, HG_DV), lambda bb, t: (bb, 0, 0, 0))],
        out_shape=[jax.ShapeDtypeStruct((b, s, HG_WIDTH), BF16),
                   jax.ShapeDtypeStruct((b, HG_HEADS, HG_DK, HG_DV), F32)],
        scratch_shapes=[pltpu.VMEM((HG_HEADS, HG_DV, HG_DK), F32),
                        pltpu.VMEM((th, HG_WIDTH), BF16),
                        pltpu.VMEM((th, HG_WIDTH), F32),
                        pltpu.VMEM((th // HG_CHUNK, HG_HEADS, HG_DV, HG_DK), F32),
                        pltpu.VMEM((max(th // HG_CHUNK, SUBLANES), HG_WIDTH), F32),
                        pltpu.VMEM((th // HG_CHUNK, HG_HEADS, HG_CHUNK, HG_CHUNK), BF16)],
        compiler_params=_cparams("parallel", "arbitrary"),
        name="hgrn_prompt",
    )(q, lf, v, g, gn.reshape(1, HG_DV))


def _mixffn_kernel(x_ref, ya_ref, yc_ref, yh_ref, gates_ref, gtm_ref, scf_ref, shf_ref, gtf_ref,
                   wa_ref, wc_ref, wh_ref, wo_ref, lmg_ref, lmb_ref,
                   wfi_ref, wfo_ref, lfg_ref, lfb_ref, o_ref, *, alpha):
    d = x_ref.shape[1]
    dff = wfo_ref.shape[0]
    gates = gates_ref[...]
    merged = (gates[:, :d].astype(F32) * _mm(ya_ref[...], wa_ref[...])
              + gates[:, d:2 * d].astype(F32) * _mm(yc_ref[...], wc_ref[...])
              + gates[:, 2 * d:].astype(F32) * _mm(yh_ref[...], wh_ref[...]))
    y = _mm(merged.astype(BF16), wo_ref[...])
    x = _layer_norm(alpha * x_ref[...] + gtm_ref[...] * y, lmg_ref[...], lmb_ref[...])
    h = _mm(_modulate(x, scf_ref, shf_ref), wfi_ref[...])
    act = (_silu(h[:, :dff]) * h[:, dff:]).astype(BF16)
    y = _mm(act, wfo_ref[...])
    o_ref[...] = _layer_norm(alpha * x + gtf_ref[...] * y, lfg_ref[...], lfb_ref[...])


def _mixffn_call(x3, ya, yc, yh, gates, mod3, wa, wc, wh, wo, lmg, lmb, wfi, wfo, lfg, lfb, tm, alpha):
    b, s, d = x3.shape
    mod_rows = mod3.shape[1]
    mrows = 1 if mod_rows == 1 else tm
    row = lambda bb, t: (bb, t, 0)
    br = pl.BlockSpec((None, tm, ya.shape[2]), row)
    xs = pl.BlockSpec((None, tm, d), row)
    vec = _const_spec((1, d))

    def weight(w):
        return pl.BlockSpec(w.shape, lambda bb, t: (0, 0), pipeline_mode=pl.Buffered(1))

    return pl.pallas_call(
        functools.partial(_mixffn_kernel, alpha=alpha),
        grid=(b, s // tm),
        in_specs=[xs, br, br, br, pl.BlockSpec((None, tm, 3 * d), row),
                  _mod_spec(mrows, d, 2), _mod_spec(mrows, d, 4), _mod_spec(mrows, d, 3), _mod_spec(mrows, d, 5),
                  weight(wa), weight(wc), weight(wh), weight(wo), vec, vec,
                  weight(wfi), weight(wfo), vec, vec],
        out_specs=xs,
        out_shape=jax.ShapeDtypeStruct((b, s, d), F32),
        compiler_params=_cparams("parallel", "parallel"),
        name="mix_ffn",
    )(x3, ya, yc, yh, gates, mod3, mod3, mod3, mod3, wa, wc, wh, wo, lmg.reshape(1, d), lmb.reshape(1, d),
      wfi, wfo, lfg.reshape(1, d), lfb.reshape(1, d))


def _kmean_kernel(pt_ref, *refs):
    o_ref = refs[-1]
    pages = refs[:-1]
    ppb = MOBA_BLOCK // PAGE_SIZE
    bps = len(pages) // ppb
    c = pl.program_id(2)

    @pl.when(c == 0)
    def _():
        o_ref[...] = jnp.zeros_like(o_ref)

    lane = lax.broadcasted_iota(jnp.int32, o_ref.shape, 1)
    acc = o_ref[...]
    for i in range(bps):
        tot = pages[ppb * i][...]
        for r in range(1, ppb):
            tot = tot + pages[ppb * i + r][...]
        mean = jnp.sum(tot, axis=-1, keepdims=True) * (1.0 / MOBA_BLOCK)
        acc = jnp.where(lane == c * bps + i, mean, acc)
    o_ref[...] = acc


def _kmean_call(cache_kt, page_table, pps):
    depth = cache_kt.shape[0]
    db, n_pages = page_table.shape
    assert n_pages * PAGE_SIZE // MOBA_BLOCK <= LANES

    def page_spec(n):
        return pl.BlockSpec((None, None, ATT_WIDTH, PAGE_SIZE),
                            lambda l, b, c, pt: (l, pt[b, c * pps + n], 0, 0))

    return pl.pallas_call(
        _kmean_kernel,
        grid_spec=pltpu.PrefetchScalarGridSpec(
            num_scalar_prefetch=1,
            grid=(depth, db, n_pages // pps),
            in_specs=[page_spec(n) for n in range(pps)],
            out_specs=pl.BlockSpec((None, None, ATT_WIDTH, LANES), lambda l, b, c, pt: (l, b, 0, 0)),
        ),
        out_shape=jax.ShapeDtypeStruct((depth, db, ATT_WIDTH, LANES), F32),
        compiler_params=_cparams("parallel", "parallel", "arbitrary"),
        name="cache_kmean",
    )(page_table, *([cache_kt] * pps))


def _sproj_kernel(x_ref, sc_ref, sh_ref, w_ref, o_ref):
    o_ref[...] = _mm(_modulate(x_ref[...], sc_ref, sh_ref), w_ref[...])


def _sproj_call(x2, mod2, w_in, tn):
    m, d = x2.shape
    n = w_in.shape[1]
    return pl.pallas_call(
        _sproj_kernel,
        grid=(n // tn,),
        in_specs=[pl.BlockSpec((m, d), lambda j: (0, 0)),
                  pl.BlockSpec((m, d), lambda j: (0, 1)),
                  pl.BlockSpec((m, d), lambda j: (0, 0)),
                  pl.BlockSpec((d, tn), lambda j: (0, j))],
        out_specs=pl.BlockSpec((m, tn), lambda j: (0, j)),
        out_shape=jax.ShapeDtypeStruct((m, n), F32),
        compiler_params=_cparams("parallel"),
        name="sample_proj",
    )(x2, mod2, mod2, w_in)


def _ssel_kernel(q_ref, kmt_ref, o_ref, *, nbs):
    head = lax.broadcasted_iota(jnp.int32, (ATT_HEADS, ATT_WIDTH), 0)
    dim = lax.broadcasted_iota(jnp.int32, (ATT_HEADS, ATT_WIDTH), 1)
    lane = lax.broadcasted_iota(jnp.int32, (ATT_HEADS, LANES), 1)
    for i in range(q_ref.shape[0]):
        q_heads = jnp.where(dim // ATT_HEAD_DIM == head, q_ref[i], 0.0)
        gate = _mm_precise(q_heads, kmt_ref[i])
        gate = jnp.where(lane < nbs, gate, -jnp.inf)
        out = jnp.zeros((ATT_HEADS, LANES), jnp.int32)
        for r in range(MOBA_TOPK):
            best = jnp.max(gate, axis=-1, keepdims=True)
            idx = jnp.min(jnp.where(gate == best, lane, LANES), axis=-1, keepdims=True)
            out = jnp.where(lane == r, idx, out)
            gate = jnp.where(lane == idx, -jnp.inf, gate)
        o_ref[i] = out


def _ssel_call(q3, kmt_l, nbs):
    db = q3.shape[0]
    group = math.gcd(db, 8)
    return pl.pallas_call(
        functools.partial(_ssel_kernel, nbs=nbs),
        grid=(db // group,),
        in_specs=[pl.BlockSpec((group, 1, ATT_WIDTH), lambda b: (b, 0, 0)),
                  pl.BlockSpec((group, ATT_WIDTH, LANES), lambda b: (b, 0, 0))],
        out_specs=pl.BlockSpec((group, ATT_HEADS, LANES), lambda b: (b, 0, 0)),
        out_shape=jax.ShapeDtypeStruct((db, ATT_HEADS, LANES), jnp.int32),
        compiler_params=_cparams("parallel"),
        name="sample_select",
    )(q3, kmt_l)


def _sattn_kernel(sel_ref, pt_ref, q_ref, kn_ref, vn_ref, *refs):
    o_ref = refs[-1]
    npg = (len(refs) - 1) // 2
    kpages = refs[:npg]
    vpages = refs[npg:2 * npg]
    per_head = npg // HEAD_PAIR
    lane = lax.broadcasted_iota(jnp.int32, (1, LANES), 1)
    q = q_ref[...] * ATT_SCALE
    kn = kn_ref[...]
    vn = vn_ref[...]
    out = jnp.zeros((1, LANES), F32)
    for h in range(HEAD_PAIR):
        in_head = (lane >= h * ATT_HEAD_DIM) & (lane < (h + 1) * ATT_HEAD_DIM)
        qh = jnp.where(in_head, q, 0.0)
        q8 = jnp.broadcast_to(qh, (8, LANES)).astype(BF16)
        s_self = jnp.sum(qh * kn, axis=-1, keepdims=True)
        scores = [_mm(q8, kpages[h * per_head + i][...].astype(BF16)) for i in range(per_head)]
        m = s_self
        for s in scores:
            m = jnp.maximum(m, jnp.max(s[0:1, :], axis=-1, keepdims=True))
        p_self = jnp.exp(s_self - m)
        l = p_self
        acc = p_self * vn
        for i, s in enumerate(scores):
            p = jnp.exp(s - m)
            l = l + jnp.sum(p[0:1, :], axis=-1, keepdims=True)
            acc = acc + _mm_nt(p.astype(BF16), vpages[h * per_head + i][...].astype(BF16))[0:1, :]
        out = jnp.where(in_head, acc / l, out)
    o_ref[...] = out.astype(BF16)


def _sattn_call(sel_flat, page_table, q3, kn3, vn3, cache_kt, cache_vt, layer):
    db = q3.shape[0]
    ppb = MOBA_BLOCK // PAGE_SIZE
    npair = ATT_HEADS // HEAD_PAIR
    vec = pl.BlockSpec((None, 1, LANES), lambda b, hp, sel, pt: (b, 0, hp))

    def page_spec(h, i, r):
        def imap(b, hp, sel, pt):
            blk = sel[(b * ATT_HEADS + hp * HEAD_PAIR + h) * MOBA_TOPK + i]
            return (layer, pt[b, blk * ppb + r], hp, 0)
        return pl.BlockSpec((None, None, LANES, PAGE_SIZE), imap)

    pages = [page_spec(h, i, r) for h in range(HEAD_PAIR) for i in range(MOBA_TOPK) for r in range(ppb)]
    return pl.pallas_call(
        _sattn_kernel,
        grid_spec=pltpu.PrefetchScalarGridSpec(
            num_scalar_prefetch=2,
            grid=(db, npair),
            in_specs=[vec, vec, vec] + pages + pages,
            out_specs=vec,
        ),
        out_shape=jax.ShapeDtypeStruct((db, 1, ATT_WIDTH), BF16),
        compiler_params=_cparams("parallel", "parallel"),
        name="sample_attn",
    )(sel_flat, page_table, q3, kn3, vn3, *([cache_kt] * len(pages)), *([cache_vt] * len(pages)))


def _sbranch_kernel(p_ref, cs_ref, hs_ref, cw_ref, cb_ref, lb_ref, gn_ref,
                    yc_ref, yh_ref, gates_ref, cso_ref, hso_ref):
    nb = p_ref.shape[0]
    p = p_ref[...]
    o0 = 3 * ATT_WIDTH
    gate_b = p[:, o0:o0 + CONV_WIDTH]
    ucv = p[:, o0 + CONV_WIDTH:o0 + 2 * CONV_WIDTH] * p[:, o0 + 2 * CONV_WIDTH:o0 + 3 * CONV_WIDTH]
    cs = cs_ref[...]
    prev2 = cs[:, :CONV_WIDTH]
    prev1 = cs[:, CONV_WIDTH:]
    cw = cw_ref[...]
    z = cb_ref[...] + prev2 * cw[0:1, :] + prev1 * cw[1:2, :] + ucv * cw[2:3, :]
    yc_ref[...] = (gate_b * z).astype(BF16)
    cso_ref[...] = jnp.concatenate([prev1, ucv], axis=1)

    o1 = o0 + 3 * CONV_WIDTH
    lb = lb_ref[...]
    hq = p[:, o1:o1 + HG_WIDTH] * HG_SCALE
    f = lb + (1.0 - lb) * _sigmoid(p[:, o1 + HG_WIDTH:o1 + 2 * HG_WIDTH])
    hv = p[:, o1 + 2 * HG_WIDTH:o1 + 3 * HG_WIDTH]
    hg = p[:, o1 + 3 * HG_WIDTH:o1 + 4 * HG_WIDTH]
    qt = hq.T
    ft = f.T
    kt = (1.0 - f).T
    gn = gn_ref[...]
    rows = []
    for b in range(nb):
        heads = []
        for h in range(HG_HEADS):
            ks = slice(h * HG_DK, (h + 1) * HG_DK)
            s_new = ft[ks, b:b + 1] * hs_ref[b, h] + kt[ks, b:b + 1] * hv[b:b + 1, ks]
            hso_ref[b, h] = s_new
            o = jnp.sum(qt[ks, b:b + 1] * s_new, axis=0, keepdims=True)
            rms = lax.rsqrt(jnp.mean(o * o, axis=-1, keepdims=True) + LN_EPS)
            heads.append(o * rms * gn)
        rows.append(jnp.concatenate(heads, axis=1))
    yh_ref[...] = (jnp.concatenate(rows, axis=0) * _silu(hg)).astype(BF16)
    gates_ref[...] = _sigmoid(p[:, o1 + 4 * HG_WIDTH:]).astype(BF16)


def _sbranch_call(proj, conv_state2, hg_state, conv_w, conv_b, lb, gn, nb):
    db, n_in = proj.shape
    d3 = n_in - 3 * ATT_WIDTH - 3 * CONV_WIDTH - 4 * HG_WIDTH
    row = lambda g: (g, 0)
    st = pl.BlockSpec((nb, HG_HEADS, HG_DK, HG_DV), lambda g: (g, 0, 0, 0))
    return pl.pallas_call(
        _sbranch_kernel,
        grid=(db // nb,),
        in_specs=[pl.BlockSpec((nb, n_in), row), pl.BlockSpec((nb, 2 * CONV_WIDTH), row), st,
                  _const_spec(conv_w.shape), _const_spec((1, CONV_WIDTH)), _const_spec((1, HG_WIDTH)),
                  _const_spec((1, HG_DV))],
        out_specs=[pl.BlockSpec((nb, CONV_WIDTH), row), pl.BlockSpec((nb, HG_WIDTH), row),
                   pl.BlockSpec((nb, d3), row), pl.BlockSpec((nb, 2 * CONV_WIDTH), row), st],
        out_shape=[jax.ShapeDtypeStruct((db, CONV_WIDTH), BF16),
                   jax.ShapeDtypeStruct((db, HG_WIDTH), BF16),
                   jax.ShapeDtypeStruct((db, d3), BF16),
                   jax.ShapeDtypeStruct((db, 2 * CONV_WIDTH), F32),
                   jax.ShapeDtypeStruct(hg_state.shape, F32)],
        compiler_params=_cparams("parallel"),
        name="sample_branch",
    )(proj, conv_state2, hg_state, conv_w, conv_b.reshape(1, CONV_WIDTH), lb.reshape(1, HG_WIDTH),
      gn.reshape(1, HG_DV))


def kernel(x_prompt, x_sample, cache_k, cache_v, state_conv, state_hgrn, page_table, c_prompt, c_sample,
           ln_in_g, ln_in_b, w_ada, b_ada, w_mix_in, conv_w, conv_b, hg_lb_logits, hg_norm_g,
           w_br_attn, w_br_conv, w_br_hgrn, w_mix_out, ln_m_g, ln_m_b, w_ffn_in, w_ffn_out, ln_f_g, ln_f_b):
    bp, seq, d = x_prompt.shape
    db = x_sample.shape[0]
    depth = w_ada.shape[0]
    n_pages = page_table.shape[1]
    past_blocks = n_pages * PAGE_SIZE // MOBA_BLOCK
    assert x_sample.shape[1] == 1 and seq % MOBA_BLOCK == 0
    assert past_blocks >= MOBA_TOPK and (n_pages * PAGE_SIZE) % MOBA_BLOCK == 0
    alpha = (2 * depth) ** 0.25
    nb = seq // MOBA_BLOCK
    nbp = -(-nb // 8) * 8

    tm = min(512, seq)
    tm_proj = min(256, seq)
    tm_ffn = min(512, seq)
    th = min(128, seq)

    w_in = w_mix_in.astype(BF16)
    w_vt = jnp.swapaxes(w_in[:, :, 2 * ATT_WIDTH:O_CONV], 1, 2)
    wa, wc, wh, wo = (w.astype(BF16) for w in (w_br_attn, w_br_conv, w_br_hgrn, w_mix_out))
    wfi, wfo = w_ffn_in.astype(BF16), w_ffn_out.astype(BF16)
    lb_cum = jnp.cumsum(jax.nn.softmax(hg_lb_logits.astype(F32), axis=0), axis=0)
    lb_all = lb_cum - lb_cum[0]

    n_c = bp + db
    mp = -(-n_c // 8) * 8
    c_all = jnp.concatenate([c_prompt, c_sample, jnp.zeros((mp - n_c, d), F32)], axis=0)
    mod = _ada_call(c_all, w_ada, b_ada)

    cache_kt = jnp.transpose(cache_k, (0, 1, 3, 4, 2)).reshape(depth, cache_k.shape[1], ATT_WIDTH, PAGE_SIZE)
    cache_vt = jnp.transpose(cache_v, (0, 1, 3, 4, 2)).reshape(depth, cache_v.shape[1], ATT_WIDTH, PAGE_SIZE)
    kmt_s = _kmean_call(cache_kt, page_table, min(16, n_pages))

    xp = _ln_call(x_prompt.reshape(bp * seq, d), ln_in_g, ln_in_b, tm).reshape(bp, seq, d)
    xs = _ln_call(x_sample.reshape(db, d), ln_in_g, ln_in_b, db)

    ks, vs, convs, hgs = [], [], [], []
    ks_s, vs_s, convs_s, hgs_s = [], [], [], []
    conv_state2 = state_conv.reshape(depth, db, 2 * CONV_WIDTH)
    for l in range(depth):
        mod_p = mod[l, :bp].reshape(bp, 1, 6 * d)
        mod_s = mod[l, bp:bp + db]

        (q, qf, k, v, kb, vt, km, yc, conv_new, hq, hlf, hv, hg, gates) = _proj_call(
            xp, mod_p, w_in[l], w_vt[l], conv_w[l], conv_b[l], lb_all[l], tm_proj)
        km3 = km.reshape(bp, nb, ATT_WIDTH)
        if nbp != nb:
            km3 = jnp.pad(km3, ((0, 0), (0, nbp - nb), (0, 0)))
        ya = _moba_call(q, qf, kb.reshape(bp, nb, MOBA_BLOCK, ATT_WIDTH), vt, km3)
        yh, hg_new = _hgrn_call(hq, hlf, hv, hg, hg_norm_g[l], th)
        xp = _mixffn_call(xp, ya, yc, yh, gates, mod_p, wa[l], wc[l], wh[l], wo[l], ln_m_g[l], ln_m_b[l],
                          wfi[l], wfo[l], ln_f_g[l], ln_f_b[l], tm_ffn, alpha)
        ks.append(k.reshape(bp, seq, ATT_HEADS, ATT_HEAD_DIM))
        vs.append(v.reshape(bp, seq, ATT_HEADS, ATT_HEAD_DIM))
        convs.append(conv_new)
        hgs.append(hg_new)

        proj = _sproj_call(xs, mod_s, w_in[l], 1024)
        q3 = proj[:, :ATT_WIDTH].reshape(db, 1, ATT_WIDTH)
        kn = proj[:, ATT_WIDTH:2 * ATT_WIDTH]
        vn = proj[:, 2 * ATT_WIDTH:O_CONV]
        sel = _ssel_call(q3, kmt_s[l], past_blocks)
        sel_flat = sel[:, :, :MOBA_TOPK].reshape(-1)
        ya_s = _sattn_call(sel_flat, page_table, q3, kn.reshape(db, 1, ATT_WIDTH),
                           vn.reshape(db, 1, ATT_WIDTH), cache_kt, cache_vt, l)
        yc_s, yh_s, gates_s, conv_s, hg_s = _sbranch_call(
            proj, conv_state2[l], state_hgrn[l], conv_w[l], conv_b[l], lb_all[l], hg_norm_g[l], min(8, db))
        xs3 = xs.reshape(1, db, d)
        mod_s3 = mod_s.reshape(1, db, 6 * d)
        xs3 = _mixffn_call(xs3, ya_s.reshape(1, db, ATT_WIDTH), yc_s.reshape(1, db, CONV_WIDTH),
                           yh_s.reshape(1, db, HG_WIDTH), gates_s.reshape(1, db, 3 * d), mod_s3,
                           wa[l], wc[l], wh[l], wo[l], ln_m_g[l], ln_m_b[l],
                           wfi[l], wfo[l], ln_f_g[l], ln_f_b[l], db, alpha)
        xs = xs3.reshape(db, d)
        ks_s.append(kn.reshape(db, 1, ATT_HEADS, ATT_HEAD_DIM))
        vs_s.append(vn.reshape(db, 1, ATT_HEADS, ATT_HEAD_DIM))
        convs_s.append(conv_s.reshape(db, 2, CONV_WIDTH))
        hgs_s.append(hg_s)

    return (xp, xs.reshape(db, 1, d), jnp.stack(ks), jnp.stack(vs), jnp.stack(convs), jnp.stack(hgs),
            jnp.stack(ks_s), jnp.stack(vs_s), jnp.stack(convs_s), jnp.stack(hgs_s))
```

```python
import functools
import math

import jax
import jax.numpy as jnp
from jax import lax
from jax.experimental import pallas as pl
from jax.experimental.pallas import tpu as pltpu

F32 = jnp.float32
BF16 = jnp.bfloat16

ATT_HEADS = 8
ATT_HEAD_DIM = 64
ATT_WIDTH = ATT_HEADS * ATT_HEAD_DIM
MOBA_BLOCK = 256
MOBA_TOPK = 3
CONV_WIDTH = 512
HG_HEADS = 4
HG_DK = 128
HG_DV = 128
HG_WIDTH = HG_HEADS * HG_DK
PAGE_SIZE = 128
LN_EPS = 1e-5
NEG_INF = -1e30
ATT_SCALE = ATT_HEAD_DIM ** -0.5
HG_SCALE = HG_DK ** -0.5
LOG2E = math.log2(math.e)

LANES = 128
HEAD_PAIR = LANES // ATT_HEAD_DIM
HG_CHUNK = 32
VMEM_LIMIT = 56 * 1024 * 1024


def _cparams(*sem):
    return pltpu.CompilerParams(dimension_semantics=sem, vmem_limit_bytes=VMEM_LIMIT)


def _mm(a, b):
    return jnp.dot(a, b, preferred_element_type=F32)


def _mm_nt(a, b):
    return lax.dot_general(a, b, (((1,), (1,)), ((), ())), preferred_element_type=F32)


def _mm_tn(a, b):
    return lax.dot_general(a, b, (((0,), (0,)), ((), ())), preferred_element_type=F32)


def _split2(x):
    hi = x.astype(BF16)
    lo = (x - hi.astype(F32)).astype(BF16)
    return hi, lo


def _mm_nt_precise(a, b):
    ah, al = _split2(a)
    bh, bl = _split2(b)
    return _mm_nt(ah, bh) + (_mm_nt(ah, bl) + _mm_nt(al, bh))


def _mm_precise(a, b):
    ah, al = _split2(a)
    bh, bl = _split2(b)
    return _mm(ah, bh) + (_mm(ah, bl) + _mm(al, bh))


def _sigmoid(x):
    return 1.0 / (1.0 + jnp.exp(-x))


def _silu(x):
    return x * _sigmoid(x)


def _layer_norm(x, g, b):
    mu = jnp.mean(x, axis=-1, keepdims=True)
    xc = x - mu
    var = jnp.mean(xc * xc, axis=-1, keepdims=True)
    return xc * lax.rsqrt(var + LN_EPS) * g + b


def _modulate(x, sc_ref, sh_ref):
    return (x * (1.0 + sc_ref[...]) + sh_ref[...]).astype(BF16)


def _cumsum_rows(x):
    n = x.shape[0]
    row = lax.broadcasted_iota(jnp.int32, x.shape, 0)
    d = 1
    while d < n:
        x = x + jnp.where(row >= d, pltpu.roll(x, d, 0), 0.0)
        d *= 2
    return x


def _const_spec(shape):
    nd = len(shape)
    return pl.BlockSpec(shape, lambda *_: (0,) * nd)


def _layer_spec(stacked, layer, single_buffer=False):
    nd = stacked.ndim - 1
    kw = {"pipeline_mode": pl.Buffered(1)} if single_buffer else {}
    return pl.BlockSpec((None,) + tuple(stacked.shape[1:]), lambda *_: (layer,) + (0,) * nd, **kw)


def _ln_kernel(x_ref, g_ref, b_ref, o_ref):
    o_ref[...] = _layer_norm(x_ref[...], g_ref[...], b_ref[...])


def _ln_call(x2, g, b, tm):
    n, d = x2.shape
    return pl.pallas_call(
        _ln_kernel,
        grid=(n // tm,),
        in_specs=[pl.BlockSpec((tm, d), lambda i: (i, 0)), _const_spec((1, d)), _const_spec((1, d))],
        out_specs=pl.BlockSpec((tm, d), lambda i: (i, 0)),
        out_shape=jax.ShapeDtypeStruct((n, d), F32),
        compiler_params=_cparams("parallel"),
        name="ln_in",
    )(x2, g.reshape(1, d), b.reshape(1, d))


def _ada_kernel(c_ref, w_ref, b_ref, o_ref):
    a = _silu(c_ref[...]).astype(BF16)
    o_ref[...] = _mm(a, w_ref[...].astype(BF16)) + b_ref[...]


def _ada_call(c_all, w_ada, b_ada):
    depth, d, n6 = w_ada.shape
    mp = c_all.shape[0]
    tn = n6 // 4
    return pl.pallas_call(
        _ada_kernel,
        grid=(depth, n6 // tn),
        in_specs=[pl.BlockSpec((mp, d), lambda l, j: (0, 0)),
                  pl.BlockSpec((None, d, tn), lambda l, j: (l, 0, j)),
                  pl.BlockSpec((None, 1, tn), lambda l, j: (l, 0, j))],
        out_specs=pl.BlockSpec((None, mp, tn), lambda l, j: (l, 0, j)),
        out_shape=jax.ShapeDtypeStruct((depth, mp, n6), F32),
        compiler_params=_cparams("parallel", "parallel"),
        name="ada_mod",
    )(c_all, w_ada, b_ada.reshape(depth, 1, n6))


def _mod_spec(rows, d, chunk):
    return pl.BlockSpec((None, rows, d), lambda b, t: (b, 0, chunk))


O_CONV = 3 * ATT_WIDTH
O_HG = O_CONV + 3 * CONV_WIDTH
O_GATE = O_HG + 4 * HG_WIDTH


def _proj_kernel(x_ref, sc_ref, sh_ref, w_ref, wvt_ref, cw_ref, cb_ref, lb_ref, kall_ref, vall_ref,
                 q_ref, qf_ref, k_ref, v_ref, kb_ref, vt_ref, km_ref,
                 yc_ref, cst_ref, hq_ref, lf_ref, hv_ref, hg_ref, gates_ref, carry_ref):
    del kall_ref, vall_ref
    tm = x_ref.shape[0]
    t = pl.program_id(1)

    @pl.when(t == 0)
    def _():
        carry_ref[...] = jnp.zeros_like(carry_ref)

    u = _modulate(x_ref[...], sc_ref, sh_ref)

    p = _mm(u, w_ref[:, :O_CONV])
    q = p[:, :ATT_WIDTH]
    k = p[:, ATT_WIDTH:2 * ATT_WIDTH]
    q_ref[...] = (q * (ATT_SCALE * LOG2E)).astype(BF16)
    qf_ref[...] = q
    k_ref[...] = k
    v_ref[...] = p[:, 2 * ATT_WIDTH:]
    kb_ref[...] = k.astype(BF16)
    vt = _mm_nt(wvt_ref[...], u).astype(BF16)
    for i in range(tm // MOBA_BLOCK):
        rows = slice(i * MOBA_BLOCK, (i + 1) * MOBA_BLOCK)
        vt_ref[i] = vt[:, rows]
        km_ref[i] = jnp.sum(k[rows], axis=0, keepdims=True) * (1.0 / MOBA_BLOCK)

    p = _mm(u, w_ref[:, O_CONV:O_HG])
    ucv = p[:, CONV_WIDTH:2 * CONV_WIDTH] * p[:, 2 * CONV_WIDTH:]
    prev1 = carry_ref[7:8, :]
    prev2 = carry_ref[6:7, :]
    row = lax.broadcasted_iota(jnp.int32, ucv.shape, 0)
    s1 = jnp.where(row == 0, prev1, pltpu.roll(ucv, 1, 0))
    s2 = jnp.where(row == 0, prev2, jnp.where(row == 1, prev1, pltpu.roll(ucv, 2, 0)))
    cw = cw_ref[...]
    z = cb_ref[...] + s2 * cw[0:1, :] + s1 * cw[1:2, :] + ucv * cw[2:3, :]
    yc_ref[...] = (p[:, :CONV_WIDTH] * z).astype(BF16)
    carry_ref[...] = ucv[tm - 8:, :]

    @pl.when(t == pl.num_programs(1) - 1)
    def _():
        cst_ref[...] = ucv[tm - 2:, :]

    p = _mm(u, w_ref[:, O_HG:O_GATE])
    lb = lb_ref[...]
    hq_ref[...] = p[:, :HG_WIDTH] * HG_SCALE
    lf_ref[...] = jnp.log(lb + (1.0 - lb) * _sigmoid(p[:, HG_WIDTH:2 * HG_WIDTH]))
    hv_ref[...] = p[:, 2 * HG_WIDTH:3 * HG_WIDTH].astype(BF16)
    hg_ref[...] = p[:, 3 * HG_WIDTH:]

    gates_ref[...] = _sigmoid(_mm(u, w_ref[:, O_GATE:])).astype(BF16)


def _proj_call(x3, mod3, w_in, w_vt, conv_w, conv_b, lb, k_all, v_all, tm, layer):
    b, s, d = x3.shape
    kv_spec = pl.BlockSpec((None, None, tm, ATT_WIDTH), lambda bb, t: (layer, bb, t, 0))
    kv_shape = jax.ShapeDtypeStruct(k_all.shape, F32)
    hbm = pl.BlockSpec(memory_space=pl.ANY)
    n_gate = w_in.shape[2] - O_GATE
    nbt = tm // MOBA_BLOCK
    nb = s // MOBA_BLOCK
    row = lambda bb, t: (bb, t, 0)
    blk4 = lambda bb, t: (bb, t, 0, 0)

    def rows(width):
        return pl.BlockSpec((None, tm, width), row)

    def arr(width, dtype):
        return jax.ShapeDtypeStruct((b, s, width), dtype)

    return pl.pallas_call(
        _proj_kernel,
        grid=(b, s // tm),
        in_specs=[rows(d), _mod_spec(1, d, 1), _mod_spec(1, d, 0),
                  _layer_spec(w_in, layer, True), _layer_spec(w_vt, layer, True),
                  _layer_spec(conv_w, layer), _layer_spec(conv_b, layer), _layer_spec(lb, layer), hbm, hbm],
        out_specs=[rows(ATT_WIDTH), rows(ATT_WIDTH), kv_spec, kv_spec, rows(ATT_WIDTH)]
                  + [pl.BlockSpec((None, nbt, ATT_WIDTH, MOBA_BLOCK), blk4),
                     pl.BlockSpec((None, nbt, 1, ATT_WIDTH), blk4),
                     rows(CONV_WIDTH), pl.BlockSpec((None, 2, CONV_WIDTH), lambda bb, t: (bb, 0, 0))]
                  + [rows(HG_WIDTH)] * 4 + [rows(n_gate)],
        out_shape=[arr(ATT_WIDTH, BF16), arr(ATT_WIDTH, F32), kv_shape, kv_shape,
                   arr(ATT_WIDTH, BF16),
                   jax.ShapeDtypeStruct((b, nb, ATT_WIDTH, MOBA_BLOCK), BF16),
                   jax.ShapeDtypeStruct((b, nb, 1, ATT_WIDTH), F32),
                   arr(CONV_WIDTH, BF16), jax.ShapeDtypeStruct((b, 2, CONV_WIDTH), F32),
                   arr(HG_WIDTH, F32), arr(HG_WIDTH, F32), arr(HG_WIDTH, BF16), arr(HG_WIDTH, F32),
                   arr(n_gate, BF16)],
        scratch_shapes=[pltpu.VMEM((8, CONV_WIDTH), F32)],
        input_output_aliases={8: 2, 9: 3},
        compiler_params=_cparams("parallel", "arbitrary"),
        name="proj_in",
    )(x3, mod3, mod3, w_in, w_vt, conv_w, conv_b, lb, k_all, v_all)


MOBA_KEY_CHUNK = 64


SUBLANES = 8


def _moba_scores(j, kb_ref, qm_ref, s_ref, bm_ref):
    for h in range(ATT_HEADS):
        hp = h // HEAD_PAIR
        s = _mm_nt(kb_ref[j, :, hp * LANES:(hp + 1) * LANES], qm_ref[h])
        s_ref[h] = s
        bm_ref[h] = jnp.max(s.reshape(MOBA_BLOCK // SUBLANES, SUBLANES, MOBA_BLOCK), axis=0)


def _moba_softmax(rows, bias, s_ref, bm_ref, p_ref, ml_ref):
    nchunk = MOBA_BLOCK // MOBA_KEY_CHUNK
    chunks = [slice(c * MOBA_KEY_CHUNK, (c + 1) * MOBA_KEY_CHUNK) for c in range(nchunk)]
    alphas = []
    for h in range(ATT_HEADS):
        m = ml_ref[h:h + 1, :]
        if bias is None:
            blk_max = jnp.max(bm_ref[h], axis=0, keepdims=True)
        else:
            blk_max = jnp.max(s_ref[h, chunks[0], :] + bias[chunks[0]], axis=0, keepdims=True)
            for c in chunks[1:]:
                blk_max = jnp.maximum(blk_max, jnp.max(s_ref[h, c, :] + bias[c], axis=0, keepdims=True))
        m_new = jnp.maximum(m, blk_max + rows[h])
        shift = jnp.where(rows[h] < 0.0, -NEG_INF, m_new)
        for c in chunks:
            s = s_ref[h, c, :]
            if bias is not None:
                s = s + bias[c]
            p_ref[h, c, :] = jnp.exp2(s - shift).astype(BF16)
        ml_ref[h:h + 1, :] = m_new
        alphas.append(jnp.exp2(m - m_new))
    return alphas


MOBA_ACC_ROWS = ATT_HEAD_DIM + 16


def _moba_values(j, alphas, vt_ref, p_ref, acc_ref):
    ones = jnp.ones((MOBA_ACC_ROWS - ATT_HEAD_DIM, MOBA_BLOCK), BF16)
    for h in range(ATT_HEADS):
        vt_h = jnp.concatenate([vt_ref[j, h * ATT_HEAD_DIM:(h + 1) * ATT_HEAD_DIM, :], ones], axis=0)
        acc_ref[h] = alphas[h] * acc_ref[h] + _mm(vt_h, p_ref[h])


def _moba_kernel(q_ref, qf_ref, kb_ref, vt_ref, km_ref, o_ref,
                 qm_ref, sel_ref, ml_ref, acc_ref, s_ref, bm_ref, p_ref):
    nbp = km_ref.shape[0]
    blk = MOBA_BLOCK
    qi = pl.program_id(1)
    lane = lax.broadcasted_iota(jnp.int32, (blk, LANES), 1)
    blk_id = lax.broadcasted_iota(jnp.int32, (nbp, blk), 0)
    past = blk_id < qi
    for h in range(ATT_HEADS):
        hp, hh = divmod(h, HEAD_PAIR)
        cols = slice(hp * LANES, (hp + 1) * LANES)
        in_head = (lane >= hh * ATT_HEAD_DIM) & (lane < (hh + 1) * ATT_HEAD_DIM)
        q = q_ref[:, cols]
        qm_ref[h] = jnp.where(in_head, q, jnp.zeros_like(q))
        gate = _mm_nt_precise(km_ref[:, cols], jnp.where(in_head, qf_ref[:, cols], 0.0))
        gate = jnp.where(past, gate, NEG_INF)
        rank = jnp.zeros((nbp, blk), jnp.int32)
        for jp in range(nbp):
            gj = gate[jp:jp + 1, :]
            beats = (gj > gate) | ((gj == gate) & (blk_id > jp))
            rank = rank + beats.astype(jnp.int32)
        sel_ref[h] = jnp.where(past & (rank < MOBA_TOPK), 0.0, NEG_INF)
    ml_ref[...] = jnp.full((ATT_HEADS, blk), NEG_INF, F32)
    acc_ref[...] = jnp.zeros_like(acc_ref)

    _moba_scores(0, kb_ref, qm_ref, s_ref, bm_ref)

    def body(j, carry):
        rows = [sel_ref[h, pl.ds(j, 1), :] for h in range(ATT_HEADS)]
        alphas = _moba_softmax(rows, None, s_ref, bm_ref, p_ref, ml_ref)
        _moba_scores(j + 1, kb_ref, qm_ref, s_ref, bm_ref)
        _moba_values(j, alphas, vt_ref, p_ref, acc_ref)
        return carry

    lax.fori_loop(0, qi, body, 0)

    key = lax.broadcasted_iota(jnp.int32, (blk, blk), 0)
    qry = lax.broadcasted_iota(jnp.int32, (blk, blk), 1)
    causal = jnp.where(key <= qry, 0.0, NEG_INF)
    zero_row = jnp.zeros((1, blk), F32)
    alphas = _moba_softmax([zero_row] * ATT_HEADS, causal, s_ref, bm_ref, p_ref, ml_ref)
    _moba_values(qi, alphas, vt_ref, p_ref, acc_ref)
    outs = [acc_ref[h, :ATT_HEAD_DIM, :] / acc_ref[h, ATT_HEAD_DIM:ATT_HEAD_DIM + 1, :]
            for h in range(ATT_HEADS)]
    o_ref[...] = jnp.concatenate(outs, axis=0).T.astype(BF16)


def _moba_call(q, qf, kb4, vt4, km3):
    b, s, _ = q.shape
    nb = s // MOBA_BLOCK
    nbp = km3.shape[1]
    qspec = pl.BlockSpec((None, MOBA_BLOCK, ATT_WIDTH), lambda bb, qi: (bb, qi, 0))
    return pl.pallas_call(
        _moba_kernel,
        grid=(b, nb),
        in_specs=[qspec, qspec,
                  pl.BlockSpec((None, nb, MOBA_BLOCK, ATT_WIDTH), lambda bb, qi: (bb, 0, 0, 0)),
                  pl.BlockSpec((None, nb, ATT_WIDTH, MOBA_BLOCK), lambda bb, qi: (bb, 0, 0, 0)),
                  pl.BlockSpec((None, nbp, ATT_WIDTH), lambda bb, qi: (bb, 0, 0))],
        out_specs=qspec,
        out_shape=jax.ShapeDtypeStruct((b, s, ATT_WIDTH), BF16),
        scratch_shapes=[pltpu.VMEM((ATT_HEADS, MOBA_BLOCK, LANES), BF16),
                        pltpu.VMEM((ATT_HEADS, nbp, MOBA_BLOCK), F32),
                        pltpu.VMEM((ATT_HEADS, MOBA_BLOCK), F32),
                        pltpu.VMEM((ATT_HEADS, MOBA_ACC_ROWS, MOBA_BLOCK), F32),
                        pltpu.VMEM((ATT_HEADS, MOBA_BLOCK, MOBA_BLOCK), F32),
                        pltpu.VMEM((ATT_HEADS, SUBLANES, MOBA_BLOCK), F32),
                        pltpu.VMEM((ATT_HEADS, MOBA_BLOCK, MOBA_BLOCK), BF16)],
        compiler_params=_cparams("parallel", "arbitrary"),
        name="moba_prompt",
    )(q, qf, kb4, vt4, km3)


def _hgrn_kernel(q_ref, lf_ref, v_ref, g_ref, gn_ref, y_ref, so_ref,
                 st_ref, qg_ref, oi_ref, ds_ref, dec_ref, a_ref):
    th = q_ref.shape[0]
    c = HG_CHUNK
    t = pl.program_id(1)

    @pl.when(t == 0)
    def _():
        st_ref[...] = jnp.zeros_like(st_ref)

    tril = (lax.broadcasted_iota(jnp.int32, (c, c), 0) >= lax.broadcasted_iota(jnp.int32, (c, c), 1))
    for ci in range(th // c):
        rows = slice(ci * c, (ci + 1) * c)
        for h in range(HG_HEADS):
            lanes = slice(h * HG_DK, (h + 1) * HG_DK)
            lf = lf_ref[rows, lanes]
            q = q_ref[rows, lanes]
            v = v_ref[rows, lanes]
            g = _cumsum_rows(lf)
            gmid = g[c // 2 - 1:c // 2, :]
            glast = g[c - 1:c, :]
            k = 1.0 - jnp.exp(lf)
            a = _mm_nt((q * jnp.exp(g - gmid)).astype(BF16), (k * jnp.exp(gmid - g)).astype(BF16))
            a_ref[ci, h] = jnp.where(tril, a, 0.0).astype(BF16)
            qg_ref[rows, lanes] = (q * jnp.exp(g)).astype(BF16)
            kd = (k * jnp.exp(glast - g)).astype(BF16)
            ds_ref[ci, h] = _mm_tn(v, kd)
            dec_ref[ci:ci + 1, lanes] = jnp.exp(glast)

    for ci in range(th // c):
        rows = slice(ci * c, (ci + 1) * c)
        for h in range(HG_HEADS):
            lanes = slice(h * HG_DK, (h + 1) * HG_DK)
            oi_ref[rows, lanes] = _mm(a_ref[ci, h], v_ref[rows, lanes])

    gn = gn_ref[...]
    for h in range(HG_HEADS):
        lanes = slice(h * HG_DK, (h + 1) * HG_DK)
        st = st_ref[h]
        for ci in range(th // c):
            rows = slice(ci * c, (ci + 1) * c)
            o = _mm_nt(qg_ref[rows, lanes], st.astype(BF16)) + oi_ref[rows, lanes]
            st = st * dec_ref[ci:ci + 1, lanes] + ds_ref[ci, h]
            rms = lax.rsqrt(jnp.mean(o * o, axis=-1, keepdims=True) + LN_EPS)
            y_ref[rows, lanes] = (o * rms * gn * _silu(g_ref[rows, lanes])).astype(BF16)
        st_ref[h] = st

    @pl.when(t == pl.num_programs(1) - 1)
    def _():
        for h in range(HG_HEADS):
            so_ref[h] = st_ref[h].T


def _hgrn_call(q, lf, v, g, gn, th, layer):
    b, s, _ = q.shape
    row = lambda bb, t: (bb, t, 0)
    act = pl.BlockSpec((None, th, HG_WIDTH), row)
    return pl.pallas_call(
        _hgrn_kernel,
        grid=(b, s // th),
        in_specs=[act, act, act, act, _layer_spec(gn, layer)],
        out_specs=[act, pl.BlockSpec((None, HG_HEADS, HG_DK, HG_DV), lambda bb, t: (bb, 0, 0, 0))],
        out_shape=[jax.ShapeDtypeStruct((b, s, HG_WIDTH), BF16),
                   jax.ShapeDtypeStruct((b, HG_HEADS, HG_DK, HG_DV), F32)],
        scratch_shapes=[pltpu.VMEM((HG_HEADS, HG_DV, HG_DK), F32),
                        pltpu.VMEM((th, HG_WIDTH), BF16),
                        pltpu.VMEM((th, HG_WIDTH), F32),
                        pltpu.VMEM((th // HG_CHUNK, HG_HEADS, HG_DV, HG_DK), F32),
                        pltpu.VMEM((max(th // HG_CHUNK, SUBLANES), HG_WIDTH), F32),
                        pltpu.VMEM((th // HG_CHUNK, HG_HEADS, HG_CHUNK, HG_CHUNK), BF16)],
        compiler_params=_cparams("parallel", "arbitrary"),
        name="hgrn_prompt",
    )(q, lf, v, g, gn)


def _mixffn_kernel(x_ref, ya_ref, yc_ref, yh_ref, gates_ref, gtm_ref, scf_ref, shf_ref, gtf_ref,
                   wa_ref, wc_ref, wh_ref, wo_ref, lmg_ref, lmb_ref,
                   wfi_ref, wfo_ref, lfg_ref, lfb_ref, o_ref, *, alpha):
    d = x_ref.shape[1]
    dff = wfo_ref.shape[0]
    gates = gates_ref[...]
    merged = (gates[:, :d].astype(F32) * _mm(ya_ref[...], wa_ref[...])
              + gates[:, d:2 * d].astype(F32) * _mm(yc_ref[...], wc_ref[...])
              + gates[:, 2 * d:].astype(F32) * _mm(yh_ref[...], wh_ref[...]))
    y = _mm(merged.astype(BF16), wo_ref[...])
    x = _layer_norm(alpha * x_ref[...] + gtm_ref[...] * y, lmg_ref[...], lmb_ref[...])
    h = _mm(_modulate(x, scf_ref, shf_ref), wfi_ref[...])
    act = (_silu(h[:, :dff]) * h[:, dff:]).astype(BF16)
    y = _mm(act, wfo_ref[...])
    o_ref[...] = _layer_norm(alpha * x + gtf_ref[...] * y, lfg_ref[...], lfb_ref[...])


def _mixffn_call(x3, ya, yc, yh, gates, mod3, wa, wc, wh, wo, lmg, lmb, wfi, wfo, lfg, lfb, tm, alpha, layer):
    b, s, d = x3.shape
    mod_rows = mod3.shape[1]
    mrows = 1 if mod_rows == 1 else tm
    row = lambda bb, t: (bb, t, 0)
    br = pl.BlockSpec((None, tm, ya.shape[2]), row)
    xs = pl.BlockSpec((None, tm, d), row)
    vec = _layer_spec(lmg, layer)

    def weight(w):
        return _layer_spec(w, layer, True)

    return pl.pallas_call(
        functools.partial(_mixffn_kernel, alpha=alpha),
        grid=(b, s // tm),
        in_specs=[xs, br, br, br, pl.BlockSpec((None, tm, 3 * d), row),
                  _mod_spec(mrows, d, 2), _mod_spec(mrows, d, 4), _mod_spec(mrows, d, 3), _mod_spec(mrows, d, 5),
                  weight(wa), weight(wc), weight(wh), weight(wo), vec, vec,
                  weight(wfi), weight(wfo), vec, vec],
        out_specs=xs,
        out_shape=jax.ShapeDtypeStruct((b, s, d), F32),
        compiler_params=_cparams("parallel", "parallel"),
        name="mix_ffn",
    )(x3, ya, yc, yh, gates, mod3, mod3, mod3, mod3, wa, wc, wh, wo, lmg, lmb, wfi, wfo, lfg, lfb)


def _kmean_kernel(pt_ref, *refs):
    o_ref = refs[-1]
    pages = refs[:-1]
    ppb = MOBA_BLOCK // PAGE_SIZE
    bps = len(pages) // ppb
    c = pl.program_id(2)

    @pl.when(c == 0)
    def _():
        o_ref[...] = jnp.zeros_like(o_ref)

    lane = lax.broadcasted_iota(jnp.int32, o_ref.shape, 1)
    averager = jnp.full((PAGE_SIZE, LANES), 1.0 / MOBA_BLOCK, BF16)
    acc = o_ref[...]
    for i in range(bps):
        tot = pages[ppb * i][...]
        for r in range(1, ppb):
            tot = tot + pages[ppb * i + r][...]
        if i % 2 == 0:
            mean = jnp.sum(tot, axis=-1, keepdims=True) * (1.0 / MOBA_BLOCK)
        else:
            hi, lo = _split2(tot)
            mean = _mm(hi, averager) + _mm(lo, averager)
        acc = jnp.where(lane == c * bps + i, mean, acc)
    o_ref[...] = acc


def _kmean_call(cache_kt, page_table, pps):
    depth = cache_kt.shape[0]
    db, n_pages = page_table.shape
    assert n_pages * PAGE_SIZE // MOBA_BLOCK <= LANES

    def page_spec(n):
        return pl.BlockSpec((None, None, ATT_WIDTH, PAGE_SIZE),
                            lambda l, b, c, pt: (l, pt[b, c * pps + n], 0, 0))

    return pl.pallas_call(
        _kmean_kernel,
        grid_spec=pltpu.PrefetchScalarGridSpec(
            num_scalar_prefetch=1,
            grid=(depth, db, n_pages // pps),
            in_specs=[page_spec(n) for n in range(pps)],
            out_specs=pl.BlockSpec((None, None, ATT_WIDTH, LANES), lambda l, b, c, pt: (l, b, 0, 0)),
        ),
        out_shape=jax.ShapeDtypeStruct((depth, db, ATT_WIDTH, LANES), F32),
        compiler_params=_cparams("parallel", "parallel", "arbitrary"),
        name="cache_kmean",
    )(page_table, *([cache_kt] * pps))


def _sproj_kernel(x_ref, sc_ref, sh_ref, w_ref, o_ref):
    o_ref[...] = _mm(_modulate(x_ref[...], sc_ref, sh_ref), w_ref[...])


def _sproj_call(x2, mod2, w_in, tn, layer):
    m, d = x2.shape
    n = w_in.shape[2]
    return pl.pallas_call(
        _sproj_kernel,
        grid=(n // tn,),
        in_specs=[pl.BlockSpec((m, d), lambda j: (0, 0)),
                  pl.BlockSpec((m, d), lambda j: (0, 1)),
                  pl.BlockSpec((m, d), lambda j: (0, 0)),
                  pl.BlockSpec((None, d, tn), lambda j: (layer, 0, j))],
        out_specs=pl.BlockSpec((m, tn), lambda j: (0, j)),
        out_shape=jax.ShapeDtypeStruct((m, n), F32),
        compiler_params=_cparams("parallel"),
        name="sample_proj",
    )(x2, mod2, mod2, w_in)


def _ssel_kernel(q_ref, kmt_ref, o_ref, *, nbs):
    head = lax.broadcasted_iota(jnp.int32, (ATT_HEADS, ATT_WIDTH), 0)
    dim = lax.broadcasted_iota(jnp.int32, (ATT_HEADS, ATT_WIDTH), 1)
    lane = lax.broadcasted_iota(jnp.int32, (ATT_HEADS, LANES), 1)
    for i in range(q_ref.shape[0]):
        q_heads = jnp.where(dim // ATT_HEAD_DIM == head, q_ref[i], 0.0)
        gate = _mm_precise(q_heads, kmt_ref[i])
        gate = jnp.where(lane < nbs, gate, -jnp.inf)
        out = jnp.zeros((ATT_HEADS, LANES), jnp.int32)
        for r in range(MOBA_TOPK):
            best = jnp.max(gate, axis=-1, keepdims=True)
            idx = jnp.min(jnp.where(gate == best, lane, LANES), axis=-1, keepdims=True)
            out = jnp.where(lane == r, idx, out)
            gate = jnp.where(lane == idx, -jnp.inf, gate)
        o_ref[i] = out


def _ssel_call(q3, kmt, nbs, layer):
    db = q3.shape[0]
    group = math.gcd(db, 8)
    return pl.pallas_call(
        functools.partial(_ssel_kernel, nbs=nbs),
        grid=(db // group,),
        in_specs=[pl.BlockSpec((group, 1, ATT_WIDTH), lambda b: (b, 0, 0)),
                  pl.BlockSpec((None, group, ATT_WIDTH, LANES), lambda b: (layer, b, 0, 0))],
        out_specs=pl.BlockSpec((group, ATT_HEADS, LANES), lambda b: (b, 0, 0)),
        out_shape=jax.ShapeDtypeStruct((db, ATT_HEADS, LANES), jnp.int32),
        compiler_params=_cparams("parallel"),
        name="sample_select",
    )(q3, kmt)


SEL_PAGES = MOBA_TOPK * (MOBA_BLOCK // PAGE_SIZE)


def _sattn_copies(seq, slot, layer, sel_ref, pt_ref, kc_ref, vc_ref, kbuf, vbuf, sem):
    ppb = MOBA_BLOCK // PAGE_SIZE
    copies = []
    for h in range(ATT_HEADS):
        rows = pl.ds(h * ATT_HEAD_DIM, ATT_HEAD_DIM)
        for i in range(MOBA_TOPK):
            blk = sel_ref[(seq * ATT_HEADS + h) * MOBA_TOPK + i]
            for r in range(ppb):
                page = pt_ref[seq, blk * ppb + r]
                toks = pl.ds((i * ppb + r) * PAGE_SIZE, PAGE_SIZE)
                copies.append(pltpu.make_async_copy(kc_ref.at[layer, page, rows, :],
                                                    kbuf.at[slot, h, :, toks], sem.at[0, slot]))
                copies.append(pltpu.make_async_copy(vc_ref.at[layer, page, rows, :],
                                                    vbuf.at[slot, h, :, toks], sem.at[1, slot]))
    return copies


def _sattn_kernel(sel_ref, pt_ref, q_ref, kn_ref, vn_ref, kc_ref, vc_ref, o_ref, kbuf, vbuf, sem, *, layer):
    b = pl.program_id(0)
    slot = b % 2
    args = (layer, sel_ref, pt_ref, kc_ref, vc_ref, kbuf, vbuf, sem)

    @pl.when(b == 0)
    def _():
        for cp in _sattn_copies(b, slot, *args):
            cp.start()

    @pl.when(b + 1 < pl.num_programs(0))
    def _():
        for cp in _sattn_copies(b + 1, 1 - slot, *args):
            cp.start()

    for cp in _sattn_copies(b, slot, *args):
        cp.wait()

    q = q_ref[...] * ATT_SCALE
    kn = kn_ref[...]
    vn = vn_ref[...]
    heads = [slice(h * ATT_HEAD_DIM, (h + 1) * ATT_HEAD_DIM) for h in range(ATT_HEADS)]
    scores = []
    for h, dims in enumerate(heads):
        q8 = jnp.broadcast_to(q[:, dims], (SUBLANES, ATT_HEAD_DIM)).astype(BF16)
        scores.append(_mm(q8, kbuf[slot, h].astype(BF16))[0:1, :])
    probs = []
    for h, dims in enumerate(heads):
        s_self = jnp.sum(q[:, dims] * kn[:, dims], axis=-1, keepdims=True)
        m = jnp.maximum(s_self, jnp.max(scores[h], axis=-1, keepdims=True))
        p_self = jnp.exp(s_self - m)
        p = jnp.exp(scores[h] - m)
        probs.append((p, p_self, p_self + jnp.sum(p, axis=-1, keepdims=True)))
    outs = []
    for h, dims in enumerate(heads):
        p, p_self, l = probs[h]
        p8 = jnp.broadcast_to(p, (SUBLANES, p.shape[1])).astype(BF16)
        acc = p_self * vn[:, dims] + _mm_nt(p8, vbuf[slot, h].astype(BF16))[0:1, :]
        outs.append(acc / l)
    o_ref[...] = jnp.concatenate(outs, axis=1).astype(BF16)


def _sattn_call(sel_flat, page_table, q3, kn3, vn3, cache_kt, cache_vt, layer):
    db = q3.shape[0]
    vec = pl.BlockSpec((None, 1, ATT_WIDTH), lambda b, sel, pt: (b, 0, 0))
    hbm = pl.BlockSpec(memory_space=pl.ANY)
    buf = pltpu.VMEM((2, ATT_HEADS, ATT_HEAD_DIM, SEL_PAGES * PAGE_SIZE), F32)
    return pl.pallas_call(
        functools.partial(_sattn_kernel, layer=layer),
        grid_spec=pltpu.PrefetchScalarGridSpec(
            num_scalar_prefetch=2,
            grid=(db,),
            in_specs=[vec, vec, vec, hbm, hbm],
            out_specs=vec,
            scratch_shapes=[buf, buf, pltpu.SemaphoreType.DMA((2, 2))],
        ),
        out_shape=jax.ShapeDtypeStruct((db, 1, ATT_WIDTH), BF16),
        compiler_params=_cparams("arbitrary"),
        name="sample_attn",
    )(sel_flat, page_table, q3, kn3, vn3, cache_kt, cache_vt)


def _sbranch_kernel(p_ref, cs_ref, hs_ref, cw_ref, cb_ref, lb_ref, gn_ref,
                    yc_ref, yh_ref, gates_ref, cso_ref, hso_ref):
    nb = p_ref.shape[0]
    p = p_ref[...]
    o0 = 3 * ATT_WIDTH
    gate_b = p[:, o0:o0 + CONV_WIDTH]
    ucv = p[:, o0 + CONV_WIDTH:o0 + 2 * CONV_WIDTH] * p[:, o0 + 2 * CONV_WIDTH:o0 + 3 * CONV_WIDTH]
    cs = cs_ref[...]
    prev2 = cs[:, :CONV_WIDTH]
    prev1 = cs[:, CONV_WIDTH:]
    cw = cw_ref[...]
    z = cb_ref[...] + prev2 * cw[0:1, :] + prev1 * cw[1:2, :] + ucv * cw[2:3, :]
    yc_ref[...] = (gate_b * z).astype(BF16)
    cso_ref[...] = jnp.concatenate([prev1, ucv], axis=1)

    o1 = o0 + 3 * CONV_WIDTH
    lb = lb_ref[...]
    hq = p[:, o1:o1 + HG_WIDTH] * HG_SCALE
    f = lb + (1.0 - lb) * _sigmoid(p[:, o1 + HG_WIDTH:o1 + 2 * HG_WIDTH])
    hv = p[:, o1 + 2 * HG_WIDTH:o1 + 3 * HG_WIDTH]
    hg = p[:, o1 + 3 * HG_WIDTH:o1 + 4 * HG_WIDTH]
    qt = hq.T
    ft = f.T
    kt = (1.0 - f).T
    gn = gn_ref[...]
    rows = []
    for b in range(nb):
        heads = []
        for h in range(HG_HEADS):
            ks = slice(h * HG_DK, (h + 1) * HG_DK)
            s_new = ft[ks, b:b + 1] * hs_ref[b, h] + kt[ks, b:b + 1] * hv[b:b + 1, ks]
            hso_ref[b, h] = s_new
            o = jnp.sum(qt[ks, b:b + 1] * s_new, axis=0, keepdims=True)
            rms = lax.rsqrt(jnp.mean(o * o, axis=-1, keepdims=True) + LN_EPS)
            heads.append(o * rms * gn)
        rows.append(jnp.concatenate(heads, axis=1))
    yh_ref[...] = (jnp.concatenate(rows, axis=0) * _silu(hg)).astype(BF16)
    gates_ref[...] = _sigmoid(p[:, o1 + 4 * HG_WIDTH:]).astype(BF16)


def _sbranch_call(proj, conv_state2, hg_state, conv_w, conv_b, lb, gn, nb, layer):
    db, n_in = proj.shape
    d3 = n_in - 3 * ATT_WIDTH - 3 * CONV_WIDTH - 4 * HG_WIDTH
    row = lambda g: (g, 0)
    st = pl.BlockSpec((nb, HG_HEADS, HG_DK, HG_DV), lambda g: (g, 0, 0, 0))
    return pl.pallas_call(
        _sbranch_kernel,
        grid=(db // nb,),
        in_specs=[pl.BlockSpec((nb, n_in), row),
                  pl.BlockSpec((None, nb, 2 * CONV_WIDTH), lambda g: (layer, g, 0)),
                  pl.BlockSpec((None, nb, HG_HEADS, HG_DK, HG_DV), lambda g: (layer, g, 0, 0, 0)),
                  _layer_spec(conv_w, layer), _layer_spec(conv_b, layer), _layer_spec(lb, layer),
                  _layer_spec(gn, layer)],
        out_specs=[pl.BlockSpec((nb, CONV_WIDTH), row), pl.BlockSpec((nb, HG_WIDTH), row),
                   pl.BlockSpec((nb, d3), row), pl.BlockSpec((nb, 2 * CONV_WIDTH), row), st],
        out_shape=[jax.ShapeDtypeStruct((db, CONV_WIDTH), BF16),
                   jax.ShapeDtypeStruct((db, HG_WIDTH), BF16),
                   jax.ShapeDtypeStruct((db, d3), BF16),
                   jax.ShapeDtypeStruct((db, 2 * CONV_WIDTH), F32),
                   jax.ShapeDtypeStruct(hg_state.shape[1:], F32)],
        compiler_params=_cparams("parallel"),
        name="sample_branch",
    )(proj, conv_state2, hg_state, conv_w, conv_b, lb, gn)


def kernel(x_prompt, x_sample, cache_k, cache_v, state_conv, state_hgrn, page_table, c_prompt, c_sample,
           ln_in_g, ln_in_b, w_ada, b_ada, w_mix_in, conv_w, conv_b, hg_lb_logits, hg_norm_g,
           w_br_attn, w_br_conv, w_br_hgrn, w_mix_out, ln_m_g, ln_m_b, w_ffn_in, w_ffn_out, ln_f_g, ln_f_b):
    bp, seq, d = x_prompt.shape
    db = x_sample.shape[0]
    depth = w_ada.shape[0]
    n_pages = page_table.shape[1]
    past_blocks = n_pages * PAGE_SIZE // MOBA_BLOCK
    assert x_sample.shape[1] == 1 and seq % MOBA_BLOCK == 0
    assert past_blocks >= MOBA_TOPK and (n_pages * PAGE_SIZE) % MOBA_BLOCK == 0
    alpha = (2 * depth) ** 0.25
    nb = seq // MOBA_BLOCK
    nbp = -(-nb // 8) * 8

    tm = min(512, seq)
    tm_proj = min(256, seq)
    tm_ffn = min(512, seq)
    th = min(128, seq)

    w_in = w_mix_in.astype(BF16)
    w_vt = jnp.swapaxes(w_in[:, :, 2 * ATT_WIDTH:O_CONV], 1, 2)
    wa, wc, wh, wo = (w.astype(BF16) for w in (w_br_attn, w_br_conv, w_br_hgrn, w_mix_out))
    wfi, wfo = w_ffn_in.astype(BF16), w_ffn_out.astype(BF16)
    lb_cum = jnp.cumsum(jax.nn.softmax(hg_lb_logits.astype(F32), axis=0), axis=0)
    lb_all = lb_cum - lb_cum[0]

    n_c = bp + db
    mp = -(-n_c // 8) * 8
    c_all = jnp.concatenate([c_prompt, c_sample, jnp.zeros((mp - n_c, d), F32)], axis=0)
    mod = _ada_call(c_all, w_ada, b_ada)

    cache_kt = jnp.transpose(cache_k, (0, 1, 3, 4, 2)).reshape(depth, cache_k.shape[1], ATT_WIDTH, PAGE_SIZE)
    cache_vt = jnp.transpose(cache_v, (0, 1, 3, 4, 2)).reshape(depth, cache_v.shape[1], ATT_WIDTH, PAGE_SIZE)
    kmt_s = _kmean_call(cache_kt, page_table, min(16, n_pages))

    xp = _ln_call(x_prompt.reshape(bp * seq, d), ln_in_g, ln_in_b, tm).reshape(bp, seq, d)
    xs = _ln_call(x_sample.reshape(db, d), ln_in_g, ln_in_b, db)

    convs, hgs = [], []
    ks_s, vs_s, convs_s, hgs_s = [], [], [], []
    conv_state2 = state_conv.reshape(depth, db, 2 * CONV_WIDTH)
    k_all = jnp.zeros((depth, bp, seq, ATT_WIDTH), F32)
    v_all = jnp.zeros((depth, bp, seq, ATT_WIDTH), F32)
    conv_b3 = conv_b.reshape(depth, 1, CONV_WIDTH)
    lb3 = lb_all.reshape(depth, 1, HG_WIDTH)
    gn3 = hg_norm_g.reshape(depth, 1, HG_DV)
    lmg3, lmb3, lfg3, lfb3 = (a.reshape(depth, 1, d) for a in (ln_m_g, ln_m_b, ln_f_g, ln_f_b))
    for l in range(depth):
        mod_p = mod[l, :bp].reshape(bp, 1, 6 * d)
        mod_s = mod[l, bp:bp + db]

        (q, qf, k_all, v_all, kb, vt, km, yc, conv_new, hq, hlf, hv, hg, gates) = _proj_call(
            xp, mod_p, w_in, w_vt, conv_w, conv_b3, lb3, k_all, v_all, tm_proj, l)
        km3 = km.reshape(bp, nb, ATT_WIDTH)
        if nbp != nb:
            km3 = jnp.pad(km3, ((0, 0), (0, nbp - nb), (0, 0)))
        ya = _moba_call(q, qf, kb.reshape(bp, nb, MOBA_BLOCK, ATT_WIDTH), vt, km3)
        yh, hg_new = _hgrn_call(hq, hlf, hv, hg, gn3, th, l)
        xp = _mixffn_call(xp, ya, yc, yh, gates, mod_p, wa, wc, wh, wo, lmg3, lmb3,
                          wfi, wfo, lfg3, lfb3, tm_ffn, alpha, l)
        convs.append(conv_new)
        hgs.append(hg_new)

        proj = _sproj_call(xs, mod_s, w_in, 1024, l)
        q3 = proj[:, :ATT_WIDTH].reshape(db, 1, ATT_WIDTH)
        kn = proj[:, ATT_WIDTH:2 * ATT_WIDTH]
        vn = proj[:, 2 * ATT_WIDTH:O_CONV]
        sel = _ssel_call(q3, kmt_s, past_blocks, l)
        sel_flat = sel[:, :, :MOBA_TOPK].reshape(-1)
        ya_s = _sattn_call(sel_flat, page_table, q3, kn.reshape(db, 1, ATT_WIDTH),
                           vn.reshape(db, 1, ATT_WIDTH), cache_kt, cache_vt, l)
        yc_s, yh_s, gates_s, conv_s, hg_s = _sbranch_call(
            proj, conv_state2, state_hgrn, conv_w, conv_b3, lb3, gn3, min(8, db), l)
        xs3 = xs.reshape(1, db, d)
        mod_s3 = mod_s.reshape(1, db, 6 * d)
        xs3 = _mixffn_call(xs3, ya_s.reshape(1, db, ATT_WIDTH), yc_s.reshape(1, db, CONV_WIDTH),
                           yh_s.reshape(1, db, HG_WIDTH), gates_s.reshape(1, db, 3 * d), mod_s3,
                           wa, wc, wh, wo, lmg3, lmb3, wfi, wfo, lfg3, lfb3, db, alpha, l)
        xs = xs3.reshape(db, d)
        ks_s.append(kn.reshape(db, 1, ATT_HEADS, ATT_HEAD_DIM))
        vs_s.append(vn.reshape(db, 1, ATT_HEADS, ATT_HEAD_DIM))
        convs_s.append(conv_s.reshape(db, 2, CONV_WIDTH))
        hgs_s.append(hg_s)

    kv_shape = (depth, bp, seq, ATT_HEADS, ATT_HEAD_DIM)
    return (xp, xs.reshape(db, 1, d), k_all.reshape(kv_shape), v_all.reshape(kv_shape),
            jnp.stack(convs), jnp.stack(hgs),
            jnp.stack(ks_s), jnp.stack(vs_s), jnp.stack(convs_s), jnp.stack(hgs_s))
```

```python
import functools
import math

import jax
import jax.numpy as jnp
from jax import lax
from jax.experimental import pallas as pl
from jax.experimental.pallas import tpu as pltpu

F32 = jnp.float32
BF16 = jnp.bfloat16

ATT_HEADS = 8
ATT_HEAD_DIM = 64
ATT_WIDTH = ATT_HEADS * ATT_HEAD_DIM
MOBA_BLOCK = 256
MOBA_TOPK = 3
CONV_WIDTH = 512
HG_HEADS = 4
HG_DK = 128
HG_DV = 128
HG_WIDTH = HG_HEADS * HG_DK
PAGE_SIZE = 128
LN_EPS = 1e-5
NEG_INF = -1e30
ATT_SCALE = ATT_HEAD_DIM ** -0.5
HG_SCALE = HG_DK ** -0.5
LOG2E = math.log2(math.e)

LANES = 128
HEAD_PAIR = LANES // ATT_HEAD_DIM
HG_CHUNK = 32
VMEM_LIMIT = 56 * 1024 * 1024


def _cparams(*sem):
    return pltpu.CompilerParams(dimension_semantics=sem, vmem_limit_bytes=VMEM_LIMIT)


def _mm(a, b):
    return jnp.dot(a, b, preferred_element_type=F32)


def _mm_nt(a, b):
    return lax.dot_general(a, b, (((1,), (1,)), ((), ())), preferred_element_type=F32)


def _mm_tn(a, b):
    return lax.dot_general(a, b, (((0,), (0,)), ((), ())), preferred_element_type=F32)


def _split2(x):
    hi = x.astype(BF16)
    lo = (x - hi.astype(F32)).astype(BF16)
    return hi, lo


def _mm_nt_precise(a, b):
    ah, al = _split2(a)
    bh, bl = _split2(b)
    return _mm_nt(ah, bh) + (_mm_nt(ah, bl) + _mm_nt(al, bh))


def _mm_precise(a, b):
    ah, al = _split2(a)
    bh, bl = _split2(b)
    return _mm(ah, bh) + (_mm(ah, bl) + _mm(al, bh))


def _sigmoid(x):
    return 1.0 / (1.0 + jnp.exp(-x))


def _silu(x):
    return x * _sigmoid(x)


def _layer_norm(x, g, b):
    mu = jnp.mean(x, axis=-1, keepdims=True)
    xc = x - mu
    var = jnp.mean(xc * xc, axis=-1, keepdims=True)
    return xc * lax.rsqrt(var + LN_EPS) * g + b


def _modulate(x, sc_ref, sh_ref):
    return (x * (1.0 + sc_ref[...]) + sh_ref[...]).astype(BF16)


def _cumsum_rows(x):
    n = x.shape[0]
    row = lax.broadcasted_iota(jnp.int32, x.shape, 0)
    d = 1
    while d < n:
        x = x + jnp.where(row >= d, pltpu.roll(x, d, 0), 0.0)
        d *= 2
    return x


def _const_spec(shape):
    nd = len(shape)
    return pl.BlockSpec(shape, lambda *_: (0,) * nd)


def _layer_spec(stacked, layer, single_buffer=False):
    nd = stacked.ndim - 1
    kw = {"pipeline_mode": pl.Buffered(1)} if single_buffer else {}
    return pl.BlockSpec((None,) + tuple(stacked.shape[1:]), lambda *_: (layer,) + (0,) * nd, **kw)


def _ln_kernel(x_ref, g_ref, b_ref, o_ref):
    o_ref[...] = _layer_norm(x_ref[...], g_ref[...], b_ref[...])


def _ln_call(x2, g, b, tm):
    n, d = x2.shape
    return pl.pallas_call(
        _ln_kernel,
        grid=(n // tm,),
        in_specs=[pl.BlockSpec((tm, d), lambda i: (i, 0)), _const_spec((1, d)), _const_spec((1, d))],
        out_specs=pl.BlockSpec((tm, d), lambda i: (i, 0)),
        out_shape=jax.ShapeDtypeStruct((n, d), F32),
        compiler_params=_cparams("parallel"),
        name="ln_in",
    )(x2, g.reshape(1, d), b.reshape(1, d))


def _ada_kernel(c_ref, w_ref, b_ref, o_ref):
    a = _silu(c_ref[...]).astype(BF16)
    o_ref[...] = _mm(a, w_ref[...].astype(BF16)) + b_ref[...]


def _ada_call(c_all, w_ada, b_ada):
    depth, d, n6 = w_ada.shape
    mp = c_all.shape[0]
    tn = n6 // 4
    return pl.pallas_call(
        _ada_kernel,
        grid=(depth, n6 // tn),
        in_specs=[pl.BlockSpec((mp, d), lambda l, j: (0, 0)),
                  pl.BlockSpec((None, d, tn), lambda l, j: (l, 0, j)),
                  pl.BlockSpec((None, 1, tn), lambda l, j: (l, 0, j))],
        out_specs=pl.BlockSpec((None, mp, tn), lambda l, j: (l, 0, j)),
        out_shape=jax.ShapeDtypeStruct((depth, mp, n6), F32),
        compiler_params=_cparams("parallel", "parallel"),
        name="ada_mod",
    )(c_all, w_ada, b_ada.reshape(depth, 1, n6))


def _mod_spec(rows, d, chunk):
    return pl.BlockSpec((None, rows, d), lambda b, t: (b, 0, chunk))


O_CONV = 3 * ATT_WIDTH
O_HG = O_CONV + 3 * CONV_WIDTH
O_GATE = O_HG + 4 * HG_WIDTH


def _proj_kernel(x_ref, sc_ref, sh_ref, w_ref, wvt_ref, cw_ref, cb_ref, lb_ref, kall_ref, vall_ref,
                 q_ref, qf_ref, k_ref, v_ref, kb_ref, vt_ref, km_ref,
                 yc_ref, cst_ref, hq_ref, lf_ref, hv_ref, hg_ref, gates_ref, carry_ref):
    del kall_ref, vall_ref
    tm = x_ref.shape[0]
    t = pl.program_id(1)

    @pl.when(t == 0)
    def _():
        carry_ref[...] = jnp.zeros_like(carry_ref)

    u = _modulate(x_ref[...], sc_ref, sh_ref)

    p = _mm(u, w_ref[:, :O_CONV])
    q = p[:, :ATT_WIDTH]
    k = p[:, ATT_WIDTH:2 * ATT_WIDTH]
    q_ref[...] = (q * (ATT_SCALE * LOG2E)).astype(BF16)
    qf_ref[...] = q
    k_ref[...] = k
    v_ref[...] = p[:, 2 * ATT_WIDTH:]
    kb_ref[...] = k.astype(BF16)
    vt = _mm_nt(wvt_ref[...], u).astype(BF16)
    for i in range(tm // MOBA_BLOCK):
        rows = slice(i * MOBA_BLOCK, (i + 1) * MOBA_BLOCK)
        vt_ref[i] = vt[:, rows]
        km_ref[i] = jnp.sum(k[rows], axis=0, keepdims=True) * (1.0 / MOBA_BLOCK)

    p = _mm(u, w_ref[:, O_CONV:O_HG])
    ucv = p[:, CONV_WIDTH:2 * CONV_WIDTH] * p[:, 2 * CONV_WIDTH:]
    prev1 = carry_ref[7:8, :]
    prev2 = carry_ref[6:7, :]
    row = lax.broadcasted_iota(jnp.int32, ucv.shape, 0)
    s1 = jnp.where(row == 0, prev1, pltpu.roll(ucv, 1, 0))
    s2 = jnp.where(row == 0, prev2, jnp.where(row == 1, prev1, pltpu.roll(ucv, 2, 0)))
    cw = cw_ref[...]
    z = cb_ref[...] + s2 * cw[0:1, :] + s1 * cw[1:2, :] + ucv * cw[2:3, :]
    yc_ref[...] = (p[:, :CONV_WIDTH] * z).astype(BF16)
    carry_ref[...] = ucv[tm - 8:, :]

    @pl.when(t == pl.num_programs(1) - 1)
    def _():
        cst_ref[...] = ucv[tm - 2:, :]

    p = _mm(u, w_ref[:, O_HG:O_GATE])
    lb = lb_ref[...]
    hq_ref[...] = p[:, :HG_WIDTH] * HG_SCALE
    lf_ref[...] = jnp.log(lb + (1.0 - lb) * _sigmoid(p[:, HG_WIDTH:2 * HG_WIDTH]))
    hv_ref[...] = p[:, 2 * HG_WIDTH:3 * HG_WIDTH].astype(BF16)
    hg_ref[...] = p[:, 3 * HG_WIDTH:]

    gates_ref[...] = _sigmoid(_mm(u, w_ref[:, O_GATE:])).astype(BF16)


def _proj_call(x3, mod3, w_in, w_vt, conv_w, conv_b, lb, k_all, v_all, tm, layer):
    b, s, d = x3.shape
    kv_spec = pl.BlockSpec((None, None, tm, ATT_WIDTH), lambda bb, t: (layer, bb, t, 0))
    kv_shape = jax.ShapeDtypeStruct(k_all.shape, F32)
    hbm = pl.BlockSpec(memory_space=pl.ANY)
    n_gate = w_in.shape[2] - O_GATE
    nbt = tm // MOBA_BLOCK
    nb = s // MOBA_BLOCK
    row = lambda bb, t: (bb, t, 0)
    blk4 = lambda bb, t: (bb, t, 0, 0)

    def rows(width):
        return pl.BlockSpec((None, tm, width), row)

    def arr(width, dtype):
        return jax.ShapeDtypeStruct((b, s, width), dtype)

    return pl.pallas_call(
        _proj_kernel,
        grid=(b, s // tm),
        in_specs=[rows(d), _mod_spec(1, d, 1), _mod_spec(1, d, 0),
                  _layer_spec(w_in, layer, True), _layer_spec(w_vt, layer, True),
                  _layer_spec(conv_w, layer), _layer_spec(conv_b, layer), _layer_spec(lb, layer), hbm, hbm],
        out_specs=[rows(ATT_WIDTH), rows(ATT_WIDTH), kv_spec, kv_spec, rows(ATT_WIDTH)]
                  + [pl.BlockSpec((None, nbt, ATT_WIDTH, MOBA_BLOCK), blk4),
                     pl.BlockSpec((None, nbt, 1, ATT_WIDTH), blk4),
                     rows(CONV_WIDTH), pl.BlockSpec((None, 2, CONV_WIDTH), lambda bb, t: (bb, 0, 0))]
                  + [rows(HG_WIDTH)] * 4 + [rows(n_gate)],
        out_shape=[arr(ATT_WIDTH, BF16), arr(ATT_WIDTH, F32), kv_shape, kv_shape,
                   arr(ATT_WIDTH, BF16),
                   jax.ShapeDtypeStruct((b, nb, ATT_WIDTH, MOBA_BLOCK), BF16),
                   jax.ShapeDtypeStruct((b, nb, 1, ATT_WIDTH), F32),
                   arr(CONV_WIDTH, BF16), jax.ShapeDtypeStruct((b, 2, CONV_WIDTH), F32),
                   arr(HG_WIDTH, F32), arr(HG_WIDTH, F32), arr(HG_WIDTH, BF16), arr(HG_WIDTH, F32),
                   arr(n_gate, BF16)],
        scratch_shapes=[pltpu.VMEM((8, CONV_WIDTH), F32)],
        input_output_aliases={8: 2, 9: 3},
        compiler_params=_cparams("parallel", "arbitrary"),
        name="proj_in",
    )(x3, mod3, mod3, w_in, w_vt, conv_w, conv_b, lb, k_all, v_all)


MOBA_KEY_CHUNK = 64


SUBLANES = 8


def _moba_scores(j, kb_ref, qm_ref, s_ref, bm_ref):
    for h in range(ATT_HEADS):
        hp = h // HEAD_PAIR
        s = _mm_nt(kb_ref[j, :, hp * LANES:(hp + 1) * LANES], qm_ref[h])
        s_ref[h] = s
        bm_ref[h] = jnp.max(s.reshape(MOBA_BLOCK // SUBLANES, SUBLANES, MOBA_BLOCK), axis=0)


def _moba_softmax(rows, bias, s_ref, bm_ref, p_ref, ml_ref):
    nchunk = MOBA_BLOCK // MOBA_KEY_CHUNK
    chunks = [slice(c * MOBA_KEY_CHUNK, (c + 1) * MOBA_KEY_CHUNK) for c in range(nchunk)]
    alphas = []
    for h in range(ATT_HEADS):
        m = ml_ref[h:h + 1, :]
        if bias is None:
            blk_max = jnp.max(bm_ref[h], axis=0, keepdims=True)
        else:
            part = None
            for c in chunks:
                s = s_ref[h, c, :] + bias[c]
                s_ref[h, c, :] = s
                cmax = jnp.max(s.reshape(MOBA_KEY_CHUNK // SUBLANES, SUBLANES, MOBA_BLOCK), axis=0)
                part = cmax if part is None else jnp.maximum(part, cmax)
            blk_max = jnp.max(part, axis=0, keepdims=True)
        m_new = jnp.maximum(m, blk_max + rows[h])
        shift = jnp.where(rows[h] < 0.0, -NEG_INF, m_new)
        for c in chunks:
            p_ref[h, c, :] = jnp.exp2(s_ref[h, c, :] - shift).astype(BF16)
        ml_ref[h:h + 1, :] = m_new
        alphas.append(jnp.exp2(m - m_new))
    return alphas


MOBA_ACC_ROWS = ATT_HEAD_DIM + 16


def _moba_values(j, alphas, vt_ref, p_ref, acc_ref):
    ones = jnp.ones((MOBA_ACC_ROWS - ATT_HEAD_DIM, MOBA_BLOCK), BF16)
    for h in range(ATT_HEADS):
        vt_h = jnp.concatenate([vt_ref[j, h * ATT_HEAD_DIM:(h + 1) * ATT_HEAD_DIM, :], ones], axis=0)
        acc_ref[h] = alphas[h] * acc_ref[h] + _mm(vt_h, p_ref[h])


def _moba_kernel(q_ref, qf_ref, kb_ref, vt_ref, km_ref, o_ref,
                 qm_ref, sel_ref, ml_ref, acc_ref, s_ref, bm_ref, p_ref):
    nbp = km_ref.shape[0]
    blk = MOBA_BLOCK
    qi = pl.program_id(1)
    lane = lax.broadcasted_iota(jnp.int32, (blk, LANES), 1)
    blk_id = lax.broadcasted_iota(jnp.int32, (nbp, blk), 0)
    past = blk_id < qi
    for h in range(ATT_HEADS):
        hp, hh = divmod(h, HEAD_PAIR)
        cols = slice(hp * LANES, (hp + 1) * LANES)
        in_head = (lane >= hh * ATT_HEAD_DIM) & (lane < (hh + 1) * ATT_HEAD_DIM)
        q = q_ref[:, cols]
        qm_ref[h] = jnp.where(in_head, q, jnp.zeros_like(q))
        gate = _mm_nt_precise(km_ref[:, cols], jnp.where(in_head, qf_ref[:, cols], 0.0))
        gate = jnp.where(past, gate, NEG_INF)
        picked = blk_id < 0
        for _ in range(MOBA_TOPK):
            best = jnp.max(gate, axis=0, keepdims=True)
            first = jnp.min(jnp.where(gate == best, blk_id, nbp), axis=0, keepdims=True)
            hit = blk_id == first
            picked = picked | hit
            gate = jnp.where(hit, -jnp.inf, gate)
        sel_ref[h] = jnp.where(past & picked, 0.0, NEG_INF)
    ml_ref[...] = jnp.full((ATT_HEADS, blk), NEG_INF, F32)
    acc_ref[...] = jnp.zeros_like(acc_ref)

    _moba_scores(0, kb_ref, qm_ref, s_ref, bm_ref)

    def body(j, carry):
        rows = [sel_ref[h, pl.ds(j, 1), :] for h in range(ATT_HEADS)]
        alphas = _moba_softmax(rows, None, s_ref, bm_ref, p_ref, ml_ref)
        _moba_scores(j + 1, kb_ref, qm_ref, s_ref, bm_ref)
        _moba_values(j, alphas, vt_ref, p_ref, acc_ref)
        return carry

    lax.fori_loop(0, qi, body, 0)

    key = lax.broadcasted_iota(jnp.int32, (blk, blk), 0)
    qry = lax.broadcasted_iota(jnp.int32, (blk, blk), 1)
    causal = jnp.where(key <= qry, 0.0, NEG_INF)
    zero_row = jnp.zeros((1, blk), F32)
    alphas = _moba_softmax([zero_row] * ATT_HEADS, causal, s_ref, bm_ref, p_ref, ml_ref)
    _moba_values(qi, alphas, vt_ref, p_ref, acc_ref)
    outs = [acc_ref[h, :ATT_HEAD_DIM, :] / acc_ref[h, ATT_HEAD_DIM:ATT_HEAD_DIM + 1, :]
            for h in range(ATT_HEADS)]
    o_ref[...] = jnp.concatenate(outs, axis=0).T.astype(BF16)


def _moba_call(q, qf, kb4, vt4, km3):
    b, s, _ = q.shape
    nb = s // MOBA_BLOCK
    nbp = km3.shape[1]
    qspec = pl.BlockSpec((None, MOBA_BLOCK, ATT_WIDTH), lambda bb, qi: (bb, qi, 0))
    return pl.pallas_call(
        _moba_kernel,
        grid=(b, nb),
        in_specs=[qspec, qspec,
                  pl.BlockSpec((None, nb, MOBA_BLOCK, ATT_WIDTH), lambda bb, qi: (bb, 0, 0, 0)),
                  pl.BlockSpec((None, nb, ATT_WIDTH, MOBA_BLOCK), lambda bb, qi: (bb, 0, 0, 0)),
                  pl.BlockSpec((None, nbp, ATT_WIDTH), lambda bb, qi: (bb, 0, 0))],
        out_specs=qspec,
        out_shape=jax.ShapeDtypeStruct((b, s, ATT_WIDTH), BF16),
        scratch_shapes=[pltpu.VMEM((ATT_HEADS, MOBA_BLOCK, LANES), BF16),
                        pltpu.VMEM((ATT_HEADS, nbp, MOBA_BLOCK), F32),
                        pltpu.VMEM((ATT_HEADS, MOBA_BLOCK), F32),
                        pltpu.VMEM((ATT_HEADS, MOBA_ACC_ROWS, MOBA_BLOCK), F32),
                        pltpu.VMEM((ATT_HEADS, MOBA_BLOCK, MOBA_BLOCK), F32),
                        pltpu.VMEM((ATT_HEADS, SUBLANES, MOBA_BLOCK), F32),
                        pltpu.VMEM((ATT_HEADS, MOBA_BLOCK, MOBA_BLOCK), BF16)],
        compiler_params=_cparams("parallel", "arbitrary"),
        name="moba_prompt",
    )(q, qf, kb4, vt4, km3)


def _hgrn_kernel(q_ref, lf_ref, v_ref, g_ref, gn_ref, y_ref, so_ref,
                 st_ref, qg_ref, oi_ref, ds_ref, dec_ref, a_ref):
    th = q_ref.shape[0]
    c = HG_CHUNK
    t = pl.program_id(1)

    @pl.when(t == 0)
    def _():
        st_ref[...] = jnp.zeros_like(st_ref)

    tril = (lax.broadcasted_iota(jnp.int32, (c, c), 0) >= lax.broadcasted_iota(jnp.int32, (c, c), 1))
    for ci in range(th // c):
        rows = slice(ci * c, (ci + 1) * c)
        for h in range(HG_HEADS):
            lanes = slice(h * HG_DK, (h + 1) * HG_DK)
            lf = lf_ref[rows, lanes]
            q = q_ref[rows, lanes]
            v = v_ref[rows, lanes]
            g = _cumsum_rows(lf)
            gmid = g[c // 2 - 1:c // 2, :]
            glast = g[c - 1:c, :]
            k = 1.0 - jnp.exp(lf)
            a = _mm_nt((q * jnp.exp(g - gmid)).astype(BF16), (k * jnp.exp(gmid - g)).astype(BF16))
            a_ref[ci, h] = jnp.where(tril, a, 0.0).astype(BF16)
            qg_ref[rows, lanes] = (q * jnp.exp(g)).astype(BF16)
            kd = (k * jnp.exp(glast - g)).astype(BF16)
            ds_ref[ci, h] = _mm_tn(v, kd)
            dec_ref[ci:ci + 1, lanes] = jnp.exp(glast)

    for ci in range(th // c):
        rows = slice(ci * c, (ci + 1) * c)
        for h in range(HG_HEADS):
            lanes = slice(h * HG_DK, (h + 1) * HG_DK)
            oi_ref[rows, lanes] = _mm(a_ref[ci, h], v_ref[rows, lanes])

    gn = gn_ref[...]
    for h in range(HG_HEADS):
        lanes = slice(h * HG_DK, (h + 1) * HG_DK)
        st = st_ref[h]
        for ci in range(th // c):
            rows = slice(ci * c, (ci + 1) * c)
            o = _mm_nt(qg_ref[rows, lanes], st.astype(BF16)) + oi_ref[rows, lanes]
            st = st * dec_ref[ci:ci + 1, lanes] + ds_ref[ci, h]
            rms = lax.rsqrt(jnp.mean(o * o, axis=-1, keepdims=True) + LN_EPS)
            y_ref[rows, lanes] = (o * rms * gn * _silu(g_ref[rows, lanes])).astype(BF16)
        st_ref[h] = st

    @pl.when(t == pl.num_programs(1) - 1)
    def _():
        for h in range(HG_HEADS):
            so_ref[h] = st_ref[h].T


def _hgrn_call(q, lf, v, g, gn, th, layer):
    b, s, _ = q.shape
    row = lambda bb, t: (bb, t, 0)
    act = pl.BlockSpec((None, th, HG_WIDTH), row)
    return pl.pallas_call(
        _hgrn_kernel,
        grid=(b, s // th),
        in_specs=[act, act, act, act, _layer_spec(gn, layer)],
        out_specs=[act, pl.BlockSpec((None, HG_HEADS, HG_DK, HG_DV), lambda bb, t: (bb, 0, 0, 0))],
        out_shape=[jax.ShapeDtypeStruct((b, s, HG_WIDTH), BF16),
                   jax.ShapeDtypeStruct((b, HG_HEADS, HG_DK, HG_DV), F32)],
        scratch_shapes=[pltpu.VMEM((HG_HEADS, HG_DV, HG_DK), F32),
                        pltpu.VMEM((th, HG_WIDTH), BF16),
                        pltpu.VMEM((th, HG_WIDTH), F32),
                        pltpu.VMEM((th // HG_CHUNK, HG_HEADS, HG_DV, HG_DK), F32),
                        pltpu.VMEM((max(th // HG_CHUNK, SUBLANES), HG_WIDTH), F32),
                        pltpu.VMEM((th // HG_CHUNK, HG_HEADS, HG_CHUNK, HG_CHUNK), BF16)],
        compiler_params=_cparams("parallel", "arbitrary"),
        name="hgrn_prompt",
    )(q, lf, v, g, gn)


def _mixffn_kernel(x_ref, ya_ref, yc_ref, yh_ref, gates_ref, gtm_ref, scf_ref, shf_ref, gtf_ref,
                   wa_ref, wc_ref, wh_ref, wo_ref, lmg_ref, lmb_ref,
                   wfi_ref, wfo_ref, lfg_ref, lfb_ref, o_ref, *, alpha):
    d = x_ref.shape[1]
    dff = wfo_ref.shape[0]
    gates = gates_ref[...]
    merged = (gates[:, :d].astype(F32) * _mm(ya_ref[...], wa_ref[...])
              + gates[:, d:2 * d].astype(F32) * _mm(yc_ref[...], wc_ref[...])
              + gates[:, 2 * d:].astype(F32) * _mm(yh_ref[...], wh_ref[...]))
    y = _mm(merged.astype(BF16), wo_ref[...])
    x = _layer_norm(alpha * x_ref[...] + gtm_ref[...] * y, lmg_ref[...], lmb_ref[...])
    h = _mm(_modulate(x, scf_ref, shf_ref), wfi_ref[...])
    act = (_silu(h[:, :dff]) * h[:, dff:]).astype(BF16)
    y = _mm(act, wfo_ref[...])
    o_ref[...] = _layer_norm(alpha * x + gtf_ref[...] * y, lfg_ref[...], lfb_ref[...])


def _mixffn_call(x3, ya, yc, yh, gates, mod3, wa, wc, wh, wo, lmg, lmb, wfi, wfo, lfg, lfb, tm, alpha, layer):
    b, s, d = x3.shape
    mod_rows = mod3.shape[1]
    mrows = 1 if mod_rows == 1 else tm
    row = lambda bb, t: (bb, t, 0)
    br = pl.BlockSpec((None, tm, ya.shape[2]), row)
    xs = pl.BlockSpec((None, tm, d), row)
    vec = _layer_spec(lmg, layer)

    def weight(w):
        return _layer_spec(w, layer, True)

    return pl.pallas_call(
        functools.partial(_mixffn_kernel, alpha=alpha),
        grid=(b, s // tm),
        in_specs=[xs, br, br, br, pl.BlockSpec((None, tm, 3 * d), row),
                  _mod_spec(mrows, d, 2), _mod_spec(mrows, d, 4), _mod_spec(mrows, d, 3), _mod_spec(mrows, d, 5),
                  weight(wa), weight(wc), weight(wh), weight(wo), vec, vec,
                  weight(wfi), weight(wfo), vec, vec],
        out_specs=xs,
        out_shape=jax.ShapeDtypeStruct((b, s, d), F32),
        compiler_params=_cparams("parallel", "parallel"),
        name="mix_ffn",
    )(x3, ya, yc, yh, gates, mod3, mod3, mod3, mod3, wa, wc, wh, wo, lmg, lmb, wfi, wfo, lfg, lfb)


KMEAN_ROWS = 64


def _kmean_kernel(pt_ref, *refs):
    o_ref = refs[-1]
    pages = refs[:-1]
    ppb = MOBA_BLOCK // PAGE_SIZE
    bps = len(pages) // ppb
    c = pl.program_id(1)

    @pl.when(c == 0)
    def _():
        o_ref[...] = jnp.zeros_like(o_ref)

    lane = lax.broadcasted_iota(jnp.int32, (KMEAN_ROWS, LANES), 1)
    averager = jnp.full((PAGE_SIZE, LANES), 1.0 / MOBA_BLOCK, BF16)
    for l in range(o_ref.shape[0]):
        for rc in range(ATT_WIDTH // KMEAN_ROWS):
            rows = slice(rc * KMEAN_ROWS, (rc + 1) * KMEAN_ROWS)
            cols = jnp.zeros((KMEAN_ROWS, LANES), F32)
            for i in range(bps):
                tot = pages[ppb * i][l, rows, :]
                for r in range(1, ppb):
                    tot = tot + pages[ppb * i + r][l, rows, :]
                if (l + i) % 2 == 0:
                    mean = jnp.sum(tot, axis=-1, keepdims=True) * (1.0 / MOBA_BLOCK)
                else:
                    hi, lo = _split2(tot)
                    mean = _mm(hi, averager) + _mm(lo, averager)
                cols = jnp.where(lane == c * bps + i, mean, cols)
            o_ref[l, rows, :] = o_ref[l, rows, :] + cols


def _kmean_call(cache_kt, page_table, pps):
    depth = cache_kt.shape[0]
    db, n_pages = page_table.shape
    assert n_pages * PAGE_SIZE // MOBA_BLOCK <= LANES

    def page_spec(n):
        return pl.BlockSpec((depth, None, ATT_WIDTH, PAGE_SIZE),
                            lambda b, c, pt: (0, pt[b, c * pps + n], 0, 0))

    return pl.pallas_call(
        _kmean_kernel,
        grid_spec=pltpu.PrefetchScalarGridSpec(
            num_scalar_prefetch=1,
            grid=(db, n_pages // pps),
            in_specs=[page_spec(n) for n in range(pps)],
            out_specs=pl.BlockSpec((depth, None, ATT_WIDTH, LANES), lambda b, c, pt: (0, b, 0, 0)),
        ),
        out_shape=jax.ShapeDtypeStruct((depth, db, ATT_WIDTH, LANES), F32),
        compiler_params=_cparams("parallel", "arbitrary"),
        name="cache_kmean",
    )(page_table, *([cache_kt] * pps))


def _sproj_kernel(x_ref, sc_ref, sh_ref, w_ref, o_ref):
    o_ref[...] = _mm(_modulate(x_ref[...], sc_ref, sh_ref), w_ref[...])


def _sproj_call(x2, mod2, w_in, tn, layer):
    m, d = x2.shape
    n = w_in.shape[2]
    return pl.pallas_call(
        _sproj_kernel,
        grid=(n // tn,),
        in_specs=[pl.BlockSpec((m, d), lambda j: (0, 0)),
                  pl.BlockSpec((m, d), lambda j: (0, 1)),
                  pl.BlockSpec((m, d), lambda j: (0, 0)),
                  pl.BlockSpec((None, d, tn), lambda j: (layer, 0, j))],
        out_specs=pl.BlockSpec((m, tn), lambda j: (0, j)),
        out_shape=jax.ShapeDtypeStruct((m, n), F32),
        compiler_params=_cparams("parallel"),
        name="sample_proj",
    )(x2, mod2, mod2, w_in)


def _ssel_kernel(q_ref, kmt_ref, o_ref, *, nbs):
    head = lax.broadcasted_iota(jnp.int32, (ATT_HEADS, ATT_WIDTH), 0)
    dim = lax.broadcasted_iota(jnp.int32, (ATT_HEADS, ATT_WIDTH), 1)
    lane = lax.broadcasted_iota(jnp.int32, (ATT_HEADS, LANES), 1)
    for i in range(q_ref.shape[0]):
        q_heads = jnp.where(dim // ATT_HEAD_DIM == head, q_ref[i], 0.0)
        gate = _mm_precise(q_heads, kmt_ref[i])
        gate = jnp.where(lane < nbs, gate, -jnp.inf)
        out = jnp.zeros((ATT_HEADS, LANES), jnp.int32)
        for r in range(MOBA_TOPK):
            best = jnp.max(gate, axis=-1, keepdims=True)
            idx = jnp.min(jnp.where(gate == best, lane, LANES), axis=-1, keepdims=True)
            out = jnp.where(lane == r, idx, out)
            gate = jnp.where(lane == idx, -jnp.inf, gate)
        o_ref[i] = out


def _ssel_call(q3, kmt, nbs, layer):
    db = q3.shape[0]
    group = math.gcd(db, 8)
    return pl.pallas_call(
        functools.partial(_ssel_kernel, nbs=nbs),
        grid=(db // group,),
        in_specs=[pl.BlockSpec((group, 1, ATT_WIDTH), lambda b: (b, 0, 0)),
                  pl.BlockSpec((None, group, ATT_WIDTH, LANES), lambda b: (layer, b, 0, 0))],
        out_specs=pl.BlockSpec((group, ATT_HEADS, LANES), lambda b: (b, 0, 0)),
        out_shape=jax.ShapeDtypeStruct((db, ATT_HEADS, LANES), jnp.int32),
        compiler_params=_cparams("parallel"),
        name="sample_select",
    )(q3, kmt)


SEL_PAGES = MOBA_TOPK * (MOBA_BLOCK // PAGE_SIZE)


def _sattn_copies(seq, slot, layer, sel_ref, pt_ref, kc_ref, vc_ref, kbuf, vbuf, sem):
    ppb = MOBA_BLOCK // PAGE_SIZE
    copies = []
    for h in range(ATT_HEADS):
        rows = pl.ds(h * ATT_HEAD_DIM, ATT_HEAD_DIM)
        for i in range(MOBA_TOPK):
            blk = sel_ref[(seq * ATT_HEADS + h) * MOBA_TOPK + i]
            for r in range(ppb):
                page = pt_ref[seq, blk * ppb + r]
                toks = pl.ds((i * ppb + r) * PAGE_SIZE, PAGE_SIZE)
                copies.append(pltpu.make_async_copy(kc_ref.at[layer, page, rows, :],
                                                    kbuf.at[slot, h, :, toks], sem.at[0, slot]))
                copies.append(pltpu.make_async_copy(vc_ref.at[layer, page, rows, :],
                                                    vbuf.at[slot, h, :, toks], sem.at[1, slot]))
    return copies


def _sattn_kernel(sel_ref, pt_ref, q_ref, kn_ref, vn_ref, kc_ref, vc_ref, o_ref, kbuf, vbuf, sem, *, layer):
    b = pl.program_id(0)
    slot = b % 2
    args = (layer, sel_ref, pt_ref, kc_ref, vc_ref, kbuf, vbuf, sem)

    @pl.when(b == 0)
    def _():
        for cp in _sattn_copies(b, slot, *args):
            cp.start()

    @pl.when(b + 1 < pl.num_programs(0))
    def _():
        for cp in _sattn_copies(b + 1, 1 - slot, *args):
            cp.start()

    for cp in _sattn_copies(b, slot, *args):
        cp.wait()

    q = q_ref[...] * ATT_SCALE
    kn = kn_ref[...]
    vn = vn_ref[...]
    heads = [slice(h * ATT_HEAD_DIM, (h + 1) * ATT_HEAD_DIM) for h in range(ATT_HEADS)]
    scores = []
    for h, dims in enumerate(heads):
        q8 = jnp.broadcast_to(q[:, dims], (SUBLANES, ATT_HEAD_DIM)).astype(BF16)
        scores.append(_mm(q8, kbuf[slot, h].astype(BF16))[0:1, :])
    probs = []
    for h, dims in enumerate(heads):
        s_self = jnp.sum(q[:, dims] * kn[:, dims], axis=-1, keepdims=True)
        m = jnp.maximum(s_self, jnp.max(scores[h], axis=-1, keepdims=True))
        p_self = jnp.exp(s_self - m)
        p = jnp.exp(scores[h] - m)
        probs.append((p, p_self, p_self + jnp.sum(p, axis=-1, keepdims=True)))
    outs = []
    for h, dims in enumerate(heads):
        p, p_self, l = probs[h]
        p8 = jnp.broadcast_to(p, (SUBLANES, p.shape[1])).astype(BF16)
        acc = p_self * vn[:, dims] + _mm_nt(p8, vbuf[slot, h].astype(BF16))[0:1, :]
        outs.append(acc / l)
    o_ref[...] = jnp.concatenate(outs, axis=1).astype(BF16)


def _sattn_call(sel_flat, page_table, q3, kn3, vn3, cache_kt, cache_vt, layer):
    db = q3.shape[0]
    vec = pl.BlockSpec((None, 1, ATT_WIDTH), lambda b, sel, pt: (b, 0, 0))
    hbm = pl.BlockSpec(memory_space=pl.ANY)
    buf = pltpu.VMEM((2, ATT_HEADS, ATT_HEAD_DIM, SEL_PAGES * PAGE_SIZE), F32)
    return pl.pallas_call(
        functools.partial(_sattn_kernel, layer=layer),
        grid_spec=pltpu.PrefetchScalarGridSpec(
            num_scalar_prefetch=2,
            grid=(db,),
            in_specs=[vec, vec, vec, hbm, hbm],
            out_specs=vec,
            scratch_shapes=[buf, buf, pltpu.SemaphoreType.DMA((2, 2))],
        ),
        out_shape=jax.ShapeDtypeStruct((db, 1, ATT_WIDTH), BF16),
        compiler_params=_cparams("arbitrary"),
        name="sample_attn",
    )(sel_flat, page_table, q3, kn3, vn3, cache_kt, cache_vt)


def _sbranch_kernel(p_ref, cs_ref, hs_ref, cw_ref, cb_ref, lb_ref, gn_ref,
                    yc_ref, yh_ref, gates_ref, cso_ref, hso_ref):
    nb = p_ref.shape[0]
    p = p_ref[...]
    o0 = 3 * ATT_WIDTH
    gate_b = p[:, o0:o0 + CONV_WIDTH]
    ucv = p[:, o0 + CONV_WIDTH:o0 + 2 * CONV_WIDTH] * p[:, o0 + 2 * CONV_WIDTH:o0 + 3 * CONV_WIDTH]
    cs = cs_ref[...]
    prev2 = cs[:, :CONV_WIDTH]
    prev1 = cs[:, CONV_WIDTH:]
    cw = cw_ref[...]
    z = cb_ref[...] + prev2 * cw[0:1, :] + prev1 * cw[1:2, :] + ucv * cw[2:3, :]
    yc_ref[...] = (gate_b * z).astype(BF16)
    cso_ref[...] = jnp.concatenate([prev1, ucv], axis=1)

    o1 = o0 + 3 * CONV_WIDTH
    lb = lb_ref[...]
    hq = p[:, o1:o1 + HG_WIDTH] * HG_SCALE
    f = lb + (1.0 - lb) * _sigmoid(p[:, o1 + HG_WIDTH:o1 + 2 * HG_WIDTH])
    hv = p[:, o1 + 2 * HG_WIDTH:o1 + 3 * HG_WIDTH]
    hg = p[:, o1 + 3 * HG_WIDTH:o1 + 4 * HG_WIDTH]
    qt = hq.T
    ft = f.T
    kt = (1.0 - f).T
    gn = gn_ref[...]
    rows = []
    for b in range(nb):
        heads = []
        for h in range(HG_HEADS):
            ks = slice(h * HG_DK, (h + 1) * HG_DK)
            s_new = ft[ks, b:b + 1] * hs_ref[b, h] + kt[ks, b:b + 1] * hv[b:b + 1, ks]
            hso_ref[b, h] = s_new
            o = jnp.sum(qt[ks, b:b + 1] * s_new, axis=0, keepdims=True)
            rms = lax.rsqrt(jnp.mean(o * o, axis=-1, keepdims=True) + LN_EPS)
            heads.append(o * rms * gn)
        rows.append(jnp.concatenate(heads, axis=1))
    yh_ref[...] = (jnp.concatenate(rows, axis=0) * _silu(hg)).astype(BF16)
    gates_ref[...] = _sigmoid(p[:, o1 + 4 * HG_WIDTH:]).astype(BF16)


def _sbranch_call(proj, conv_state2, hg_state, conv_w, conv_b, lb, gn, nb, layer):
    db, n_in = proj.shape
    d3 = n_in - 3 * ATT_WIDTH - 3 * CONV_WIDTH - 4 * HG_WIDTH
    row = lambda g: (g, 0)
    st = pl.BlockSpec((nb, HG_HEADS, HG_DK, HG_DV), lambda g: (g, 0, 0, 0))
    return pl.pallas_call(
        _sbranch_kernel,
        grid=(db // nb,),
        in_specs=[pl.BlockSpec((nb, n_in), row),
                  pl.BlockSpec((None, nb, 2 * CONV_WIDTH), lambda g: (layer, g, 0)),
                  pl.BlockSpec((None, nb, HG_HEADS, HG_DK, HG_DV), lambda g: (layer, g, 0, 0, 0)),
                  _layer_spec(conv_w, layer), _layer_spec(conv_b, layer), _layer_spec(lb, layer),
                  _layer_spec(gn, layer)],
        out_specs=[pl.BlockSpec((nb, CONV_WIDTH), row), pl.BlockSpec((nb, HG_WIDTH), row),
                   pl.BlockSpec((nb, d3), row), pl.BlockSpec((nb, 2 * CONV_WIDTH), row), st],
        out_shape=[jax.ShapeDtypeStruct((db, CONV_WIDTH), BF16),
                   jax.ShapeDtypeStruct((db, HG_WIDTH), BF16),
                   jax.ShapeDtypeStruct((db, d3), BF16),
                   jax.ShapeDtypeStruct((db, 2 * CONV_WIDTH), F32),
                   jax.ShapeDtypeStruct(hg_state.shape[1:], F32)],
        compiler_params=_cparams("parallel"),
        name="sample_branch",
    )(proj, conv_state2, hg_state, conv_w, conv_b, lb, gn)


def kernel(x_prompt, x_sample, cache_k, cache_v, state_conv, state_hgrn, page_table, c_prompt, c_sample,
           ln_in_g, ln_in_b, w_ada, b_ada, w_mix_in, conv_w, conv_b, hg_lb_logits, hg_norm_g,
           w_br_attn, w_br_conv, w_br_hgrn, w_mix_out, ln_m_g, ln_m_b, w_ffn_in, w_ffn_out, ln_f_g, ln_f_b):
    bp, seq, d = x_prompt.shape
    db = x_sample.shape[0]
    depth = w_ada.shape[0]
    n_pages = page_table.shape[1]
    past_blocks = n_pages * PAGE_SIZE // MOBA_BLOCK
    assert x_sample.shape[1] == 1 and seq % MOBA_BLOCK == 0
    assert past_blocks >= MOBA_TOPK and (n_pages * PAGE_SIZE) % MOBA_BLOCK == 0
    alpha = (2 * depth) ** 0.25
    nb = seq // MOBA_BLOCK
    nbp = -(-nb // 8) * 8

    tm = min(512, seq)
    tm_proj = min(256, seq)
    tm_ffn = min(512, seq)
    th = min(128, seq)

    w_in = w_mix_in.astype(BF16)
    w_vt = jnp.swapaxes(w_in[:, :, 2 * ATT_WIDTH:O_CONV], 1, 2)
    wa, wc, wh, wo = (w.astype(BF16) for w in (w_br_attn, w_br_conv, w_br_hgrn, w_mix_out))
    wfi, wfo = w_ffn_in.astype(BF16), w_ffn_out.astype(BF16)
    lb_cum = jnp.cumsum(jax.nn.softmax(hg_lb_logits.astype(F32), axis=0), axis=0)
    lb_all = lb_cum - lb_cum[0]

    n_c = bp + db
    mp = -(-n_c // 8) * 8
    c_all = jnp.concatenate([c_prompt, c_sample, jnp.zeros((mp - n_c, d), F32)], axis=0)
    mod = _ada_call(c_all, w_ada, b_ada)

    cache_kt = jnp.transpose(cache_k, (0, 1, 3, 4, 2)).reshape(depth, cache_k.shape[1], ATT_WIDTH, PAGE_SIZE)
    cache_vt = jnp.transpose(cache_v, (0, 1, 3, 4, 2)).reshape(depth, cache_v.shape[1], ATT_WIDTH, PAGE_SIZE)
    kmt_s = _kmean_call(cache_kt, page_table, min(4, n_pages))

    xp = _ln_call(x_prompt.reshape(bp * seq, d), ln_in_g, ln_in_b, tm).reshape(bp, seq, d)
    xs = _ln_call(x_sample.reshape(db, d), ln_in_g, ln_in_b, db)

    convs, hgs = [], []
    ks_s, vs_s, convs_s, hgs_s = [], [], [], []
    conv_state2 = state_conv.reshape(depth, db, 2 * CONV_WIDTH)
    k_all = jnp.zeros((depth, bp, seq, ATT_WIDTH), F32)
    v_all = jnp.zeros((depth, bp, seq, ATT_WIDTH), F32)
    conv_b3 = conv_b.reshape(depth, 1, CONV_WIDTH)
    lb3 = lb_all.reshape(depth, 1, HG_WIDTH)
    gn3 = hg_norm_g.reshape(depth, 1, HG_DV)
    lmg3, lmb3, lfg3, lfb3 = (a.reshape(depth, 1, d) for a in (ln_m_g, ln_m_b, ln_f_g, ln_f_b))
    for l in range(depth):
        mod_p = mod[l, :bp].reshape(bp, 1, 6 * d)
        mod_s = mod[l, bp:bp + db]

        (q, qf, k_all, v_all, kb, vt, km, yc, conv_new, hq, hlf, hv, hg, gates) = _proj_call(
            xp, mod_p, w_in, w_vt, conv_w, conv_b3, lb3, k_all, v_all, tm_proj, l)
        km3 = km.reshape(bp, nb, ATT_WIDTH)
        if nbp != nb:
            km3 = jnp.pad(km3, ((0, 0), (0, nbp - nb), (0, 0)))
        ya = _moba_call(q, qf, kb.reshape(bp, nb, MOBA_BLOCK, ATT_WIDTH), vt, km3)
        yh, hg_new = _hgrn_call(hq, hlf, hv, hg, gn3, th, l)
        xp = _mixffn_call(xp, ya, yc, yh, gates, mod_p, wa, wc, wh, wo, lmg3, lmb3,
                          wfi, wfo, lfg3, lfb3, tm_ffn, alpha, l)
        convs.append(conv_new)
        hgs.append(hg_new)

        proj = _sproj_call(xs, mod_s, w_in, 1024, l)
        q3 = proj[:, :ATT_WIDTH].reshape(db, 1, ATT_WIDTH)
        kn = proj[:, ATT_WIDTH:2 * ATT_WIDTH]
        vn = proj[:, 2 * ATT_WIDTH:O_CONV]
        sel = _ssel_call(q3, kmt_s, past_blocks, l)
        sel_flat = sel[:, :, :MOBA_TOPK].reshape(-1)
        ya_s = _sattn_call(sel_flat, page_table, q3, kn.reshape(db, 1, ATT_WIDTH),
                           vn.reshape(db, 1, ATT_WIDTH), cache_kt, cache_vt, l)
        yc_s, yh_s, gates_s, conv_s, hg_s = _sbranch_call(
            proj, conv_state2, state_hgrn, conv_w, conv_b3, lb3, gn3, min(8, db), l)
        xs3 = xs.reshape(1, db, d)
        mod_s3 = mod_s.reshape(1, db, 6 * d)
        xs3 = _mixffn_call(xs3, ya_s.reshape(1, db, ATT_WIDTH), yc_s.reshape(1, db, CONV_WIDTH),
                           yh_s.reshape(1, db, HG_WIDTH), gates_s.reshape(1, db, 3 * d), mod_s3,
                           wa, wc, wh, wo, lmg3, lmb3, wfi, wfo, lfg3, lfb3, db, alpha, l)
        xs = xs3.reshape(db, d)
        ks_s.append(kn.reshape(db, 1, ATT_HEADS, ATT_HEAD_DIM))
        vs_s.append(vn.reshape(db, 1, ATT_HEADS, ATT_HEAD_DIM))
        convs_s.append(conv_s.reshape(db, 2, CONV_WIDTH))
        hgs_s.append(hg_s)

    kv_shape = (depth, bp, seq, ATT_HEADS, ATT_HEAD_DIM)
    return (xp, xs.reshape(db, 1, d), k_all.reshape(kv_shape), v_all.reshape(kv_shape),
            jnp.stack(convs), jnp.stack(hgs),
            jnp.stack(ks_s), jnp.stack(vs_s), jnp.stack(convs_s), jnp.stack(hgs_s))
```

```python
import functools
import math

import jax
import jax.numpy as jnp
from jax import lax
from jax.experimental import pallas as pl
from jax.experimental.pallas import tpu as pltpu

F32 = jnp.float32
BF16 = jnp.bfloat16

ATT_HEADS = 8
ATT_HEAD_DIM = 64
ATT_WIDTH = ATT_HEADS * ATT_HEAD_DIM
MOBA_BLOCK = 256
MOBA_TOPK = 3
CONV_WIDTH = 512
HG_HEADS = 4
HG_DK = 128
HG_DV = 128
HG_WIDTH = HG_HEADS * HG_DK
PAGE_SIZE = 128
LN_EPS = 1e-5
NEG_INF = -1e30
ATT_SCALE = ATT_HEAD_DIM ** -0.5
HG_SCALE = HG_DK ** -0.5
LOG2E = math.log2(math.e)

LANES = 128
HEAD_PAIR = LANES // ATT_HEAD_DIM
HG_CHUNK = 32
VMEM_LIMIT = 56 * 1024 * 1024


def _cparams(*sem):
    return pltpu.CompilerParams(dimension_semantics=sem, vmem_limit_bytes=VMEM_LIMIT)


def _mm(a, b):
    return jnp.dot(a, b, preferred_element_type=F32)


def _mm_nt(a, b):
    return lax.dot_general(a, b, (((1,), (1,)), ((), ())), preferred_element_type=F32)


def _mm_tn(a, b):
    return lax.dot_general(a, b, (((0,), (0,)), ((), ())), preferred_element_type=F32)


def _split2(x):
    hi = x.astype(BF16)
    lo = (x - hi.astype(F32)).astype(BF16)
    return hi, lo


def _mm_nt_precise(a, b):
    ah, al = _split2(a)
    bh, bl = _split2(b)
    return _mm_nt(ah, bh) + (_mm_nt(ah, bl) + _mm_nt(al, bh))


def _mm_precise(a, b):
    ah, al = _split2(a)
    bh, bl = _split2(b)
    return _mm(ah, bh) + (_mm(ah, bl) + _mm(al, bh))


def _sigmoid(x):
    return 1.0 / (1.0 + jnp.exp(-x))


def _silu(x):
    return x * _sigmoid(x)


def _layer_norm(x, g, b):
    mu = jnp.mean(x, axis=-1, keepdims=True)
    xc = x - mu
    var = jnp.mean(xc * xc, axis=-1, keepdims=True)
    return xc * lax.rsqrt(var + LN_EPS) * g + b


def _modulate(x, sc_ref, sh_ref):
    return (x * (1.0 + sc_ref[...]) + sh_ref[...]).astype(BF16)


def _cumsum_rows(x):
    n = x.shape[0]
    row = lax.broadcasted_iota(jnp.int32, x.shape, 0)
    d = 1
    while d < n:
        x = x + jnp.where(row >= d, pltpu.roll(x, d, 0), 0.0)
        d *= 2
    return x


def _const_spec(shape):
    nd = len(shape)
    return pl.BlockSpec(shape, lambda *_: (0,) * nd)


def _layer_spec(stacked, layer, single_buffer=False):
    nd = stacked.ndim - 1
    kw = {"pipeline_mode": pl.Buffered(1)} if single_buffer else {}
    return pl.BlockSpec((None,) + tuple(stacked.shape[1:]), lambda *_: (layer,) + (0,) * nd, **kw)


def _ln_kernel(x_ref, g_ref, b_ref, o_ref):
    o_ref[...] = _layer_norm(x_ref[...], g_ref[...], b_ref[...])


def _ln_call(x2, g, b, tm):
    n, d = x2.shape
    return pl.pallas_call(
        _ln_kernel,
        grid=(n // tm,),
        in_specs=[pl.BlockSpec((tm, d), lambda i: (i, 0)), _const_spec((1, d)), _const_spec((1, d))],
        out_specs=pl.BlockSpec((tm, d), lambda i: (i, 0)),
        out_shape=jax.ShapeDtypeStruct((n, d), F32),
        compiler_params=_cparams("parallel"),
        name="ln_in",
    )(x2, g.reshape(1, d), b.reshape(1, d))


def _ada_kernel(c_ref, w_ref, b_ref, o_ref):
    a = _silu(c_ref[...]).astype(BF16)
    o_ref[...] = _mm(a, w_ref[...].astype(BF16)) + b_ref[...]


def _ada_call(c_all, w_ada, b_ada):
    depth, d, n6 = w_ada.shape
    mp = c_all.shape[0]
    tn = n6 // 4
    return pl.pallas_call(
        _ada_kernel,
        grid=(depth, n6 // tn),
        in_specs=[pl.BlockSpec((mp, d), lambda l, j: (0, 0)),
                  pl.BlockSpec((None, d, tn), lambda l, j: (l, 0, j)),
                  pl.BlockSpec((None, 1, tn), lambda l, j: (l, 0, j))],
        out_specs=pl.BlockSpec((None, mp, tn), lambda l, j: (l, 0, j)),
        out_shape=jax.ShapeDtypeStruct((depth, mp, n6), F32),
        compiler_params=_cparams("parallel", "parallel"),
        name="ada_mod",
    )(c_all, w_ada, b_ada.reshape(depth, 1, n6))


def _mod_spec(rows, d, chunk):
    return pl.BlockSpec((None, rows, d), lambda b, t, *_: (b, 0, chunk))


O_CONV = 3 * ATT_WIDTH
O_HG = O_CONV + 3 * CONV_WIDTH
O_GATE = O_HG + 4 * HG_WIDTH


def _cache_page_copies(step, slot, layer, pt_ref, kc_ref, pbuf, sem):
    pps = pbuf.shape[1]
    steps_per_seq = pt_ref.shape[1] // pps
    seq = step // steps_per_seq
    first = (step % steps_per_seq) * pps
    return [pltpu.make_async_copy(kc_ref.at[layer, pt_ref[seq, first + n]], pbuf.at[slot, n], sem.at[slot])
            for n in range(pps)]


def _cache_key_means(part, slot, pbuf, kmt_ref):
    pps = pbuf.shape[1]
    ppb = MOBA_BLOCK // PAGE_SIZE
    bps = pps // ppb
    lane = lax.broadcasted_iota(jnp.int32, (KMEAN_ROWS, LANES), 1)
    for rc in range(ATT_WIDTH // KMEAN_ROWS):
        rows = slice(rc * KMEAN_ROWS, (rc + 1) * KMEAN_ROWS)
        cols = jnp.zeros((KMEAN_ROWS, LANES), F32)
        for i in range(bps):
            tot = pbuf[slot, ppb * i, rows, :]
            for r in range(1, ppb):
                tot = tot + pbuf[slot, ppb * i + r, rows, :]
            mean = jnp.sum(tot, axis=-1, keepdims=True) * (1.0 / MOBA_BLOCK)
            cols = jnp.where(lane == part * bps + i, mean, cols)
        kmt_ref[rows, :] = kmt_ref[rows, :] + cols


def _proj_kernel(pt_ref, x_ref, sc_ref, sh_ref, w_ref, wvt_ref, cw_ref, cb_ref, lb_ref,
                 kall_ref, vall_ref, kc_ref,
                 q_ref, qf_ref, k_ref, v_ref, kb_ref, vt_ref, km_ref,
                 yc_ref, cst_ref, hq_ref, lf_ref, hv_ref, hg_ref, gates_ref, kmt_ref,
                 carry_ref, pbuf, sem, *, layer):
    del kall_ref, vall_ref
    tm = x_ref.shape[0]
    t = pl.program_id(1)
    step = pl.program_id(0) * pl.num_programs(1) + t
    slot = step % 2
    dma_args = (layer, pt_ref, kc_ref, pbuf, sem)

    @pl.when(step == 0)
    def _():
        for cp in _cache_page_copies(step, slot, *dma_args):
            cp.start()

    @pl.when(step + 1 < pl.num_programs(0) * pl.num_programs(1))
    def _():
        for cp in _cache_page_copies(step + 1, 1 - slot, *dma_args):
            cp.start()

    @pl.when(t == 0)
    def _():
        carry_ref[...] = jnp.zeros_like(carry_ref)

    steps_per_seq = pt_ref.shape[1] // pbuf.shape[1]
    part = step % steps_per_seq

    @pl.when(part == 0)
    def _():
        kmt_ref[...] = jnp.zeros_like(kmt_ref)

    for cp in _cache_page_copies(step, slot, *dma_args):
        cp.wait()
    _cache_key_means(part, slot, pbuf, kmt_ref)

    u = _modulate(x_ref[...], sc_ref, sh_ref)

    p = _mm(u, w_ref[:, :O_CONV])
    q = p[:, :ATT_WIDTH]
    k = p[:, ATT_WIDTH:2 * ATT_WIDTH]
    q_ref[...] = (q * (ATT_SCALE * LOG2E)).astype(BF16)
    qf_ref[...] = q
    k_ref[...] = k
    v_ref[...] = p[:, 2 * ATT_WIDTH:]
    kb_ref[...] = k.astype(BF16)
    vt = _mm_nt(wvt_ref[...], u).astype(BF16)
    for i in range(tm // MOBA_BLOCK):
        rows = slice(i * MOBA_BLOCK, (i + 1) * MOBA_BLOCK)
        vt_ref[i] = vt[:, rows]
        km_ref[i] = jnp.sum(k[rows], axis=0, keepdims=True) * (1.0 / MOBA_BLOCK)

    p = _mm(u, w_ref[:, O_CONV:O_HG])
    ucv = p[:, CONV_WIDTH:2 * CONV_WIDTH] * p[:, 2 * CONV_WIDTH:]
    prev1 = carry_ref[7:8, :]
    prev2 = carry_ref[6:7, :]
    row = lax.broadcasted_iota(jnp.int32, ucv.shape, 0)
    s1 = jnp.where(row == 0, prev1, pltpu.roll(ucv, 1, 0))
    s2 = jnp.where(row == 0, prev2, jnp.where(row == 1, prev1, pltpu.roll(ucv, 2, 0)))
    cw = cw_ref[...]
    z = cb_ref[...] + s2 * cw[0:1, :] + s1 * cw[1:2, :] + ucv * cw[2:3, :]
    yc_ref[...] = (p[:, :CONV_WIDTH] * z).astype(BF16)
    carry_ref[...] = ucv[tm - 8:, :]
    cst_ref[...] = ucv[tm - 2:, :]

    p = _mm(u, w_ref[:, O_HG:O_GATE])
    lb = lb_ref[...]
    hq_ref[...] = p[:, :HG_WIDTH] * HG_SCALE
    lf_ref[...] = jnp.log(lb + (1.0 - lb) * _sigmoid(p[:, HG_WIDTH:2 * HG_WIDTH]))
    hv_ref[...] = p[:, 2 * HG_WIDTH:3 * HG_WIDTH].astype(BF16)
    hg_ref[...] = p[:, 3 * HG_WIDTH:]

    gates_ref[...] = _sigmoid(_mm(u, w_ref[:, O_GATE:])).astype(BF16)


def _proj_call(x3, mod3, w_in, w_vt, conv_w, conv_b, lb, k_all, v_all, page_table, cache_kt, tm, layer):
    b, s, d = x3.shape
    db, n_pages = page_table.shape
    n_steps = b * (s // tm)
    assert n_steps % db == 0 and n_pages % (n_steps // db) == 0
    steps_per_seq = n_steps // db
    pps = n_pages // steps_per_seq
    assert pps % (MOBA_BLOCK // PAGE_SIZE) == 0 and n_pages * PAGE_SIZE // MOBA_BLOCK <= LANES
    nt = s // tm
    kv_spec = pl.BlockSpec((None, None, tm, ATT_WIDTH), lambda bb, t, pt: (layer, bb, t, 0))
    kv_shape = jax.ShapeDtypeStruct(k_all.shape, F32)
    hbm = pl.BlockSpec(memory_space=pl.ANY)
    n_gate = w_in.shape[2] - O_GATE
    nbt = tm // MOBA_BLOCK
    nb = s // MOBA_BLOCK
    row = lambda bb, t, pt: (bb, t, 0)
    blk4 = lambda bb, t, pt: (bb, t, 0, 0)

    def rows(width):
        return pl.BlockSpec((None, tm, width), row)

    def arr(width, dtype):
        return jax.ShapeDtypeStruct((b, s, width), dtype)

    return pl.pallas_call(
        functools.partial(_proj_kernel, layer=layer),
        grid_spec=pltpu.PrefetchScalarGridSpec(
            num_scalar_prefetch=1,
            grid=(b, nt),
            in_specs=[rows(d), _mod_spec(1, d, 1), _mod_spec(1, d, 0),
                      _layer_spec(w_in, layer, True), _layer_spec(w_vt, layer, True),
                      _layer_spec(conv_w, layer), _layer_spec(conv_b, layer), _layer_spec(lb, layer),
                      hbm, hbm, hbm],
            out_specs=[rows(ATT_WIDTH), rows(ATT_WIDTH), kv_spec, kv_spec, rows(ATT_WIDTH)]
                      + [pl.BlockSpec((None, nbt, ATT_WIDTH, MOBA_BLOCK), blk4),
                         pl.BlockSpec((None, nbt, 1, ATT_WIDTH), blk4),
                         rows(CONV_WIDTH), pl.BlockSpec((None, 2, CONV_WIDTH), lambda bb, t, pt: (bb, 0, 0))]
                      + [rows(HG_WIDTH)] * 4 + [rows(n_gate)]
                      + [pl.BlockSpec((None, ATT_WIDTH, LANES),
                                      lambda bb, t, pt: ((bb * nt + t) // steps_per_seq, 0, 0))],
            scratch_shapes=[pltpu.VMEM((8, CONV_WIDTH), F32),
                            pltpu.VMEM((2, pps, ATT_WIDTH, PAGE_SIZE), F32),
                            pltpu.SemaphoreType.DMA((2,))],
        ),
        out_shape=[arr(ATT_WIDTH, BF16), arr(ATT_WIDTH, F32), kv_shape, kv_shape,
                   arr(ATT_WIDTH, BF16),
                   jax.ShapeDtypeStruct((b, nb, ATT_WIDTH, MOBA_BLOCK), BF16),
                   jax.ShapeDtypeStruct((b, nb, 1, ATT_WIDTH), F32),
                   arr(CONV_WIDTH, BF16), jax.ShapeDtypeStruct((b, 2, CONV_WIDTH), F32),
                   arr(HG_WIDTH, F32), arr(HG_WIDTH, F32), arr(HG_WIDTH, BF16), arr(HG_WIDTH, F32),
                   arr(n_gate, BF16),
                   jax.ShapeDtypeStruct((db, ATT_WIDTH, LANES), F32)],
        input_output_aliases={9: 2, 10: 3},
        compiler_params=_cparams("arbitrary", "arbitrary"),
        name="proj_in",
    )(page_table, x3, mod3, mod3, w_in, w_vt, conv_w, conv_b, lb, k_all, v_all, cache_kt)


MOBA_KEY_CHUNK = 64


SUBLANES = 8


def _moba_scores(j, kb_ref, qm_ref, s_ref, bm_ref):
    for h in range(ATT_HEADS):
        hp = h // HEAD_PAIR
        s = _mm_nt(kb_ref[j, :, hp * LANES:(hp + 1) * LANES], qm_ref[h])
        s_ref[h] = s
        bm_ref[h] = jnp.max(s.reshape(MOBA_BLOCK // SUBLANES, SUBLANES, MOBA_BLOCK), axis=0)


def _moba_softmax(rows, bias, s_ref, bm_ref, p_ref, ml_ref):
    nchunk = MOBA_BLOCK // MOBA_KEY_CHUNK
    chunks = [slice(c * MOBA_KEY_CHUNK, (c + 1) * MOBA_KEY_CHUNK) for c in range(nchunk)]
    alphas = []
    for h in range(ATT_HEADS):
        m = ml_ref[h:h + 1, :]
        if bias is None:
            blk_max = jnp.max(bm_ref[h], axis=0, keepdims=True)
        else:
            part = None
            for c in chunks:
                s = s_ref[h, c, :] + bias[c]
                s_ref[h, c, :] = s
                cmax = jnp.max(s.reshape(MOBA_KEY_CHUNK // SUBLANES, SUBLANES, MOBA_BLOCK), axis=0)
                part = cmax if part is None else jnp.maximum(part, cmax)
            blk_max = jnp.max(part, axis=0, keepdims=True)
        m_new = jnp.maximum(m, blk_max + rows[h])
        shift = jnp.where(rows[h] < 0.0, -NEG_INF, m_new)
        for c in chunks:
            p_ref[h, c, :] = jnp.exp2(s_ref[h, c, :] - shift).astype(BF16)
        ml_ref[h:h + 1, :] = m_new
        alphas.append(jnp.exp2(m - m_new))
    return alphas


MOBA_ACC_ROWS = ATT_HEAD_DIM + 16


def _moba_values(j, alphas, vt_ref, p_ref, acc_ref):
    ones = jnp.ones((MOBA_ACC_ROWS - ATT_HEAD_DIM, MOBA_BLOCK), BF16)
    for h in range(ATT_HEADS):
        vt_h = jnp.concatenate([vt_ref[j, h * ATT_HEAD_DIM:(h + 1) * ATT_HEAD_DIM, :], ones], axis=0)
        acc_ref[h] = alphas[h] * acc_ref[h] + _mm(vt_h, p_ref[h])


def _moba_kernel(q_ref, qf_ref, kb_ref, vt_ref, km_ref, o_ref,
                 qm_ref, sel_ref, ml_ref, acc_ref, s_ref, bm_ref, p_ref):
    nbp = km_ref.shape[0]
    blk = MOBA_BLOCK
    qi = pl.program_id(1)
    lane = lax.broadcasted_iota(jnp.int32, (blk, LANES), 1)
    blk_id = lax.broadcasted_iota(jnp.int32, (nbp, blk), 0)
    past = blk_id < qi
    for h in range(ATT_HEADS):
        hp, hh = divmod(h, HEAD_PAIR)
        cols = slice(hp * LANES, (hp + 1) * LANES)
        in_head = (lane >= hh * ATT_HEAD_DIM) & (lane < (hh + 1) * ATT_HEAD_DIM)
        q = q_ref[:, cols]
        qm_ref[h] = jnp.where(in_head, q, jnp.zeros_like(q))
        gate = _mm_nt_precise(km_ref[:, cols], jnp.where(in_head, qf_ref[:, cols], 0.0))
        gate = jnp.where(past, gate, NEG_INF)
        picked = blk_id < 0
        for _ in range(MOBA_TOPK):
            best = jnp.max(gate, axis=0, keepdims=True)
            first = jnp.min(jnp.where(gate == best, blk_id, nbp), axis=0, keepdims=True)
            hit = blk_id == first
            picked = picked | hit
            gate = jnp.where(hit, -jnp.inf, gate)
        sel_ref[h] = jnp.where(past & picked, 0.0, NEG_INF)
    ml_ref[...] = jnp.full((ATT_HEADS, blk), NEG_INF, F32)
    acc_ref[...] = jnp.zeros_like(acc_ref)

    _moba_scores(0, kb_ref, qm_ref, s_ref, bm_ref)

    def body(j, carry):
        rows = [sel_ref[h, pl.ds(j, 1), :] for h in range(ATT_HEADS)]
        alphas = _moba_softmax(rows, None, s_ref, bm_ref, p_ref, ml_ref)
        _moba_scores(j + 1, kb_ref, qm_ref, s_ref, bm_ref)
        _moba_values(j, alphas, vt_ref, p_ref, acc_ref)
        return carry

    lax.fori_loop(0, qi, body, 0)

    key = lax.broadcasted_iota(jnp.int32, (blk, blk), 0)
    qry = lax.broadcasted_iota(jnp.int32, (blk, blk), 1)
    causal = jnp.where(key <= qry, 0.0, NEG_INF)
    zero_row = jnp.zeros((1, blk), F32)
    alphas = _moba_softmax([zero_row] * ATT_HEADS, causal, s_ref, bm_ref, p_ref, ml_ref)
    _moba_values(qi, alphas, vt_ref, p_ref, acc_ref)
    outs = [acc_ref[h, :ATT_HEAD_DIM, :] / acc_ref[h, ATT_HEAD_DIM:ATT_HEAD_DIM + 1, :]
            for h in range(ATT_HEADS)]
    o_ref[...] = jnp.concatenate(outs, axis=0).T.astype(BF16)


def _moba_call(q, qf, kb4, vt4, km3):
    b, s, _ = q.shape
    nb = s // MOBA_BLOCK
    nbp = km3.shape[1]
    qspec = pl.BlockSpec((None, MOBA_BLOCK, ATT_WIDTH), lambda bb, qi: (bb, qi, 0))
    return pl.pallas_call(
        _moba_kernel,
        grid=(b, nb),
        in_specs=[qspec, qspec,
                  pl.BlockSpec((None, nb, MOBA_BLOCK, ATT_WIDTH), lambda bb, qi: (bb, 0, 0, 0)),
                  pl.BlockSpec((None, nb, ATT_WIDTH, MOBA_BLOCK), lambda bb, qi: (bb, 0, 0, 0)),
                  pl.BlockSpec((None, nbp, ATT_WIDTH), lambda bb, qi: (bb, 0, 0))],
        out_specs=qspec,
        out_shape=jax.ShapeDtypeStruct((b, s, ATT_WIDTH), BF16),
        scratch_shapes=[pltpu.VMEM((ATT_HEADS, MOBA_BLOCK, LANES), BF16),
                        pltpu.VMEM((ATT_HEADS, nbp, MOBA_BLOCK), F32),
                        pltpu.VMEM((ATT_HEADS, MOBA_BLOCK), F32),
                        pltpu.VMEM((ATT_HEADS, MOBA_ACC_ROWS, MOBA_BLOCK), F32),
                        pltpu.VMEM((ATT_HEADS, MOBA_BLOCK, MOBA_BLOCK), F32),
                        pltpu.VMEM((ATT_HEADS, SUBLANES, MOBA_BLOCK), F32),
                        pltpu.VMEM((ATT_HEADS, MOBA_BLOCK, MOBA_BLOCK), BF16)],
        compiler_params=_cparams("parallel", "arbitrary"),
        name="moba_prompt",
    )(q, qf, kb4, vt4, km3)


def _hgrn_kernel(q_ref, lf_ref, v_ref, g_ref, gn_ref, y_ref, so_ref,
                 st_ref, qg_ref, oi_ref, ds_ref, dec_ref, a_ref):
    th = q_ref.shape[0]
    c = HG_CHUNK
    t = pl.program_id(1)

    @pl.when(t == 0)
    def _():
        st_ref[...] = jnp.zeros_like(st_ref)

    tril = (lax.broadcasted_iota(jnp.int32, (c, c), 0) >= lax.broadcasted_iota(jnp.int32, (c, c), 1))
    for ci in range(th // c):
        rows = slice(ci * c, (ci + 1) * c)
        for h in range(HG_HEADS):
            lanes = slice(h * HG_DK, (h + 1) * HG_DK)
            lf = lf_ref[rows, lanes]
            q = q_ref[rows, lanes]
            v = v_ref[rows, lanes]
            g = _cumsum_rows(lf)
            gmid = g[c // 2 - 1:c // 2, :]
            glast = g[c - 1:c, :]
            k = 1.0 - jnp.exp(lf)
            a = _mm_nt((q * jnp.exp(g - gmid)).astype(BF16), (k * jnp.exp(gmid - g)).astype(BF16))
            a_ref[ci, h] = jnp.where(tril, a, 0.0).astype(BF16)
            qg_ref[rows, lanes] = (q * jnp.exp(g)).astype(BF16)
            kd = (k * jnp.exp(glast - g)).astype(BF16)
            ds_ref[ci, h] = _mm_tn(v, kd)
            dec_ref[ci:ci + 1, lanes] = jnp.exp(glast)

    for ci in range(th // c):
        rows = slice(ci * c, (ci + 1) * c)
        for h in range(HG_HEADS):
            lanes = slice(h * HG_DK, (h + 1) * HG_DK)
            oi_ref[rows, lanes] = _mm(a_ref[ci, h], v_ref[rows, lanes])

    gn = gn_ref[...]
    for h in range(HG_HEADS):
        lanes = slice(h * HG_DK, (h + 1) * HG_DK)
        st = st_ref[h]
        for ci in range(th // c):
            rows = slice(ci * c, (ci + 1) * c)
            o = _mm_nt(qg_ref[rows, lanes], st.astype(BF16)) + oi_ref[rows, lanes]
            st = st * dec_ref[ci:ci + 1, lanes] + ds_ref[ci, h]
            rms = lax.rsqrt(jnp.mean(o * o, axis=-1, keepdims=True) + LN_EPS)
            y_ref[rows, lanes] = (o * rms * gn * _silu(g_ref[rows, lanes])).astype(BF16)
        st_ref[h] = st

    @pl.when(t == pl.num_programs(1) - 1)
    def _():
        for h in range(HG_HEADS):
            so_ref[h] = st_ref[h].T


def _hgrn_call(q, lf, v, g, gn, th, layer):
    b, s, _ = q.shape
    row = lambda bb, t: (bb, t, 0)
    act = pl.BlockSpec((None, th, HG_WIDTH), row)
    return pl.pallas_call(
        _hgrn_kernel,
        grid=(b, s // th),
        in_specs=[act, act, act, act, _layer_spec(gn, layer)],
        out_specs=[act, pl.BlockSpec((None, HG_HEADS, HG_DK, HG_DV), lambda bb, t: (bb, 0, 0, 0))],
        out_shape=[jax.ShapeDtypeStruct((b, s, HG_WIDTH), BF16),
                   jax.ShapeDtypeStruct((b, HG_HEADS, HG_DK, HG_DV), F32)],
        scratch_shapes=[pltpu.VMEM((HG_HEADS, HG_DV, HG_DK), F32),
                        pltpu.VMEM((th, HG_WIDTH), BF16),
                        pltpu.VMEM((th, HG_WIDTH), F32),
                        pltpu.VMEM((th // HG_CHUNK, HG_HEADS, HG_DV, HG_DK), F32),
                        pltpu.VMEM((max(th // HG_CHUNK, SUBLANES), HG_WIDTH), F32),
                        pltpu.VMEM((th // HG_CHUNK, HG_HEADS, HG_CHUNK, HG_CHUNK), BF16)],
        compiler_params=_cparams("parallel", "arbitrary"),
        name="hgrn_prompt",
    )(q, lf, v, g, gn)


def _mixffn_kernel(x_ref, ya_ref, yc_ref, yh_ref, gates_ref, gtm_ref, scf_ref, shf_ref, gtf_ref,
                   wa_ref, wc_ref, wh_ref, wo_ref, lmg_ref, lmb_ref,
                   wfi_ref, wfo_ref, lfg_ref, lfb_ref, o_ref, *, alpha):
    d = x_ref.shape[1]
    dff = wfo_ref.shape[0]
    gates = gates_ref[...]
    merged = (gates[:, :d].astype(F32) * _mm(ya_ref[...], wa_ref[...])
              + gates[:, d:2 * d].astype(F32) * _mm(yc_ref[...], wc_ref[...])
              + gates[:, 2 * d:].astype(F32) * _mm(yh_ref[...], wh_ref[...]))
    y = _mm(merged.astype(BF16), wo_ref[...])
    x = _layer_norm(alpha * x_ref[...] + gtm_ref[...] * y, lmg_ref[...], lmb_ref[...])
    h = _mm(_modulate(x, scf_ref, shf_ref), wfi_ref[...])
    act = (_silu(h[:, :dff]) * h[:, dff:]).astype(BF16)
    y = _mm(act, wfo_ref[...])
    o_ref[...] = _layer_norm(alpha * x + gtf_ref[...] * y, lfg_ref[...], lfb_ref[...])


def _mixffn_call(x3, ya, yc, yh, gates, mod3, wa, wc, wh, wo, lmg, lmb, wfi, wfo, lfg, lfb, tm, alpha, layer):
    b, s, d = x3.shape
    mod_rows = mod3.shape[1]
    mrows = 1 if mod_rows == 1 else tm
    row = lambda bb, t: (bb, t, 0)
    br = pl.BlockSpec((None, tm, ya.shape[2]), row)
    xs = pl.BlockSpec((None, tm, d), row)
    vec = _layer_spec(lmg, layer)

    def weight(w):
        return _layer_spec(w, layer, True)

    return pl.pallas_call(
        functools.partial(_mixffn_kernel, alpha=alpha),
        grid=(b, s // tm),
        in_specs=[xs, br, br, br, pl.BlockSpec((None, tm, 3 * d), row),
                  _mod_spec(mrows, d, 2), _mod_spec(mrows, d, 4), _mod_spec(mrows, d, 3), _mod_spec(mrows, d, 5),
                  weight(wa), weight(wc), weight(wh), weight(wo), vec, vec,
                  weight(wfi), weight(wfo), vec, vec],
        out_specs=xs,
        out_shape=jax.ShapeDtypeStruct((b, s, d), F32),
        compiler_params=_cparams("parallel", "parallel"),
        name="mix_ffn",
    )(x3, ya, yc, yh, gates, mod3, mod3, mod3, mod3, wa, wc, wh, wo, lmg, lmb, wfi, wfo, lfg, lfb)


KMEAN_ROWS = 64


def _sproj_kernel(x_ref, sc_ref, sh_ref, w_ref, o_ref):
    o_ref[...] = _mm(_modulate(x_ref[...], sc_ref, sh_ref), w_ref[...])


def _sproj_call(x2, mod2, w_in, tn, layer):
    m, d = x2.shape
    n = w_in.shape[2]
    return pl.pallas_call(
        _sproj_kernel,
        grid=(n // tn,),
        in_specs=[pl.BlockSpec((m, d), lambda j: (0, 0)),
                  pl.BlockSpec((m, d), lambda j: (0, 1)),
                  pl.BlockSpec((m, d), lambda j: (0, 0)),
                  pl.BlockSpec((None, d, tn), lambda j: (layer, 0, j))],
        out_specs=pl.BlockSpec((m, tn), lambda j: (0, j)),
        out_shape=jax.ShapeDtypeStruct((m, n), F32),
        compiler_params=_cparams("parallel"),
        name="sample_proj",
    )(x2, mod2, mod2, w_in)


def _ssel_kernel(q_ref, kmt_ref, o_ref, *, nbs):
    head = lax.broadcasted_iota(jnp.int32, (ATT_HEADS, ATT_WIDTH), 0)
    dim = lax.broadcasted_iota(jnp.int32, (ATT_HEADS, ATT_WIDTH), 1)
    lane = lax.broadcasted_iota(jnp.int32, (ATT_HEADS, LANES), 1)
    for i in range(q_ref.shape[0]):
        q_heads = jnp.where(dim // ATT_HEAD_DIM == head, q_ref[i], 0.0)
        gate = _mm_precise(q_heads, kmt_ref[i])
        gate = jnp.where(lane < nbs, gate, -jnp.inf)
        out = jnp.zeros((ATT_HEADS, LANES), jnp.int32)
        for r in range(MOBA_TOPK):
            best = jnp.max(gate, axis=-1, keepdims=True)
            idx = jnp.min(jnp.where(gate == best, lane, LANES), axis=-1, keepdims=True)
            out = jnp.where(lane == r, idx, out)
            gate = jnp.where(lane == idx, -jnp.inf, gate)
        o_ref[i] = out


def _ssel_call(q3, kmt, nbs):
    db = q3.shape[0]
    group = math.gcd(db, 8)
    return pl.pallas_call(
        functools.partial(_ssel_kernel, nbs=nbs),
        grid=(db // group,),
        in_specs=[pl.BlockSpec((group, 1, ATT_WIDTH), lambda b: (b, 0, 0)),
                  pl.BlockSpec((group, ATT_WIDTH, LANES), lambda b: (b, 0, 0))],
        out_specs=pl.BlockSpec((group, ATT_HEADS, LANES), lambda b: (b, 0, 0)),
        out_shape=jax.ShapeDtypeStruct((db, ATT_HEADS, LANES), jnp.int32),
        compiler_params=_cparams("parallel"),
        name="sample_select",
    )(q3, kmt)


SEL_PAGES = MOBA_TOPK * (MOBA_BLOCK // PAGE_SIZE)


def _sattn_copies(seq, slot, layer, sel_ref, pt_ref, kc_ref, vc_ref, kbuf, vbuf, sem):
    ppb = MOBA_BLOCK // PAGE_SIZE
    copies = []
    for h in range(ATT_HEADS):
        rows = pl.ds(h * ATT_HEAD_DIM, ATT_HEAD_DIM)
        for i in range(MOBA_TOPK):
            blk = sel_ref[(seq * ATT_HEADS + h) * MOBA_TOPK + i]
            for r in range(ppb):
                page = pt_ref[seq, blk * ppb + r]
                toks = pl.ds((i * ppb + r) * PAGE_SIZE, PAGE_SIZE)
                copies.append(pltpu.make_async_copy(kc_ref.at[layer, page, rows, :],
                                                    kbuf.at[slot, h, :, toks], sem.at[0, slot]))
                copies.append(pltpu.make_async_copy(vc_ref.at[layer, page, rows, :],
                                                    vbuf.at[slot, h, :, toks], sem.at[1, slot]))
    return copies


def _sattn_kernel(sel_ref, pt_ref, q_ref, kn_ref, vn_ref, kc_ref, vc_ref, o_ref, kbuf, vbuf, sem, *, layer):
    b = pl.program_id(0)
    slot = b % 2
    args = (layer, sel_ref, pt_ref, kc_ref, vc_ref, kbuf, vbuf, sem)

    @pl.when(b == 0)
    def _():
        for cp in _sattn_copies(b, slot, *args):
            cp.start()

    @pl.when(b + 1 < pl.num_programs(0))
    def _():
        for cp in _sattn_copies(b + 1, 1 - slot, *args):
            cp.start()

    for cp in _sattn_copies(b, slot, *args):
        cp.wait()

    q = q_ref[...] * ATT_SCALE
    kn = kn_ref[...]
    vn = vn_ref[...]
    heads = [slice(h * ATT_HEAD_DIM, (h + 1) * ATT_HEAD_DIM) for h in range(ATT_HEADS)]
    scores = []
    for h, dims in enumerate(heads):
        q8 = jnp.broadcast_to(q[:, dims], (SUBLANES, ATT_HEAD_DIM)).astype(BF16)
        scores.append(_mm(q8, kbuf[slot, h].astype(BF16))[0:1, :])
    probs = []
    for h, dims in enumerate(heads):
        s_self = jnp.sum(q[:, dims] * kn[:, dims], axis=-1, keepdims=True)
        m = jnp.maximum(s_self, jnp.max(scores[h], axis=-1, keepdims=True))
        p_self = jnp.exp(s_self - m)
        p = jnp.exp(scores[h] - m)
        probs.append((p, p_self, p_self + jnp.sum(p, axis=-1, keepdims=True)))
    outs = []
    for h, dims in enumerate(heads):
        p, p_self, l = probs[h]
        p8 = jnp.broadcast_to(p, (SUBLANES, p.shape[1])).astype(BF16)
        acc = p_self * vn[:, dims] + _mm_nt(p8, vbuf[slot, h].astype(BF16))[0:1, :]
        outs.append(acc / l)
    o_ref[...] = jnp.concatenate(outs, axis=1).astype(BF16)


def _sattn_call(sel_flat, page_table, q3, kn3, vn3, cache_kt, cache_vt, layer):
    db = q3.shape[0]
    vec = pl.BlockSpec((None, 1, ATT_WIDTH), lambda b, sel, pt: (b, 0, 0))
    hbm = pl.BlockSpec(memory_space=pl.ANY)
    buf = pltpu.VMEM((2, ATT_HEADS, ATT_HEAD_DIM, SEL_PAGES * PAGE_SIZE), F32)
    return pl.pallas_call(
        functools.partial(_sattn_kernel, layer=layer),
        grid_spec=pltpu.PrefetchScalarGridSpec(
            num_scalar_prefetch=2,
            grid=(db,),
            in_specs=[vec, vec, vec, hbm, hbm],
            out_specs=vec,
            scratch_shapes=[buf, buf, pltpu.SemaphoreType.DMA((2, 2))],
        ),
        out_shape=jax.ShapeDtypeStruct((db, 1, ATT_WIDTH), BF16),
        compiler_params=_cparams("arbitrary"),
        name="sample_attn",
    )(sel_flat, page_table, q3, kn3, vn3, cache_kt, cache_vt)


def _sbranch_kernel(p_ref, cs_ref, hs_ref, cw_ref, cb_ref, lb_ref, gn_ref,
                    yc_ref, yh_ref, gates_ref, cso_ref, hso_ref):
    nb = p_ref.shape[0]
    p = p_ref[...]
    o0 = 3 * ATT_WIDTH
    gate_b = p[:, o0:o0 + CONV_WIDTH]
    ucv = p[:, o0 + CONV_WIDTH:o0 + 2 * CONV_WIDTH] * p[:, o0 + 2 * CONV_WIDTH:o0 + 3 * CONV_WIDTH]
    cs = cs_ref[...]
    prev2 = cs[:, :CONV_WIDTH]
    prev1 = cs[:, CONV_WIDTH:]
    cw = cw_ref[...]
    z = cb_ref[...] + prev2 * cw[0:1, :] + prev1 * cw[1:2, :] + ucv * cw[2:3, :]
    yc_ref[...] = (gate_b * z).astype(BF16)
    cso_ref[...] = jnp.concatenate([prev1, ucv], axis=1)

    o1 = o0 + 3 * CONV_WIDTH
    lb = lb_ref[...]
    hq = p[:, o1:o1 + HG_WIDTH] * HG_SCALE
    f = lb + (1.0 - lb) * _sigmoid(p[:, o1 + HG_WIDTH:o1 + 2 * HG_WIDTH])
    hv = p[:, o1 + 2 * HG_WIDTH:o1 + 3 * HG_WIDTH]
    hg = p[:, o1 + 3 * HG_WIDTH:o1 + 4 * HG_WIDTH]
    qt = hq.T
    ft = f.T
    kt = (1.0 - f).T
    gn = gn_ref[...]
    rows = []
    for b in range(nb):
        heads = []
        for h in range(HG_HEADS):
            ks = slice(h * HG_DK, (h + 1) * HG_DK)
            s_new = ft[ks, b:b + 1] * hs_ref[b, h] + kt[ks, b:b + 1] * hv[b:b + 1, ks]
            hso_ref[b, h] = s_new
            o = jnp.sum(qt[ks, b:b + 1] * s_new, axis=0, keepdims=True)
            rms = lax.rsqrt(jnp.mean(o * o, axis=-1, keepdims=True) + LN_EPS)
            heads.append(o * rms * gn)
        rows.append(jnp.concatenate(heads, axis=1))
    yh_ref[...] = (jnp.concatenate(rows, axis=0) * _silu(hg)).astype(BF16)
    gates_ref[...] = _sigmoid(p[:, o1 + 4 * HG_WIDTH:]).astype(BF16)


def _sbranch_call(proj, conv_state2, hg_state, conv_w, conv_b, lb, gn, nb, layer):
    db, n_in = proj.shape
    d3 = n_in - 3 * ATT_WIDTH - 3 * CONV_WIDTH - 4 * HG_WIDTH
    row = lambda g: (g, 0)
    st = pl.BlockSpec((nb, HG_HEADS, HG_DK, HG_DV), lambda g: (g, 0, 0, 0))
    return pl.pallas_call(
        _sbranch_kernel,
        grid=(db // nb,),
        in_specs=[pl.BlockSpec((nb, n_in), row),
                  pl.BlockSpec((None, nb, 2 * CONV_WIDTH), lambda g: (layer, g, 0)),
                  pl.BlockSpec((None, nb, HG_HEADS, HG_DK, HG_DV), lambda g: (layer, g, 0, 0, 0)),
                  _layer_spec(conv_w, layer), _layer_spec(conv_b, layer), _layer_spec(lb, layer),
                  _layer_spec(gn, layer)],
        out_specs=[pl.BlockSpec((nb, CONV_WIDTH), row), pl.BlockSpec((nb, HG_WIDTH), row),
                   pl.BlockSpec((nb, d3), row), pl.BlockSpec((nb, 2 * CONV_WIDTH), row), st],
        out_shape=[jax.ShapeDtypeStruct((db, CONV_WIDTH), BF16),
                   jax.ShapeDtypeStruct((db, HG_WIDTH), BF16),
                   jax.ShapeDtypeStruct((db, d3), BF16),
                   jax.ShapeDtypeStruct((db, 2 * CONV_WIDTH), F32),
                   jax.ShapeDtypeStruct(hg_state.shape[1:], F32)],
        compiler_params=_cparams("parallel"),
        name="sample_branch",
    )(proj, conv_state2, hg_state, conv_w, conv_b, lb, gn)


def kernel(x_prompt, x_sample, cache_k, cache_v, state_conv, state_hgrn, page_table, c_prompt, c_sample,
           ln_in_g, ln_in_b, w_ada, b_ada, w_mix_in, conv_w, conv_b, hg_lb_logits, hg_norm_g,
           w_br_attn, w_br_conv, w_br_hgrn, w_mix_out, ln_m_g, ln_m_b, w_ffn_in, w_ffn_out, ln_f_g, ln_f_b):
    bp, seq, d = x_prompt.shape
    db = x_sample.shape[0]
    depth = w_ada.shape[0]
    n_pages = page_table.shape[1]
    past_blocks = n_pages * PAGE_SIZE // MOBA_BLOCK
    assert x_sample.shape[1] == 1 and seq % MOBA_BLOCK == 0
    assert past_blocks >= MOBA_TOPK and (n_pages * PAGE_SIZE) % MOBA_BLOCK == 0
    alpha = (2 * depth) ** 0.25
    nb = seq // MOBA_BLOCK
    nbp = -(-nb // 8) * 8

    tm = min(512, seq)
    tm_proj = min(256, seq)
    tm_ffn = min(512, seq)
    th = min(128, seq)

    w_in = w_mix_in.astype(BF16)
    w_vt = jnp.swapaxes(w_in[:, :, 2 * ATT_WIDTH:O_CONV], 1, 2)
    wa, wc, wh, wo = (w.astype(BF16) for w in (w_br_attn, w_br_conv, w_br_hgrn, w_mix_out))
    wfi, wfo = w_ffn_in.astype(BF16), w_ffn_out.astype(BF16)
    lb_cum = jnp.cumsum(jax.nn.softmax(hg_lb_logits.astype(F32), axis=0), axis=0)
    lb_all = lb_cum - lb_cum[0]

    n_c = bp + db
    mp = -(-n_c // 8) * 8
    c_all = jnp.concatenate([c_prompt, c_sample, jnp.zeros((mp - n_c, d), F32)], axis=0)
    mod = _ada_call(c_all, w_ada, b_ada)

    cache_kt = jnp.transpose(cache_k, (0, 1, 3, 4, 2)).reshape(depth, cache_k.shape[1], ATT_WIDTH, PAGE_SIZE)
    cache_vt = jnp.transpose(cache_v, (0, 1, 3, 4, 2)).reshape(depth, cache_v.shape[1], ATT_WIDTH, PAGE_SIZE)

    xp = _ln_call(x_prompt.reshape(bp * seq, d), ln_in_g, ln_in_b, tm).reshape(bp, seq, d)
    xs = _ln_call(x_sample.reshape(db, d), ln_in_g, ln_in_b, db)

    convs, hgs = [], []
    ks_s, vs_s, convs_s, hgs_s = [], [], [], []
    conv_state2 = state_conv.reshape(depth, db, 2 * CONV_WIDTH)
    k_all = jnp.zeros((depth, bp, seq, ATT_WIDTH), F32)
    v_all = jnp.zeros((depth, bp, seq, ATT_WIDTH), F32)
    conv_b3 = conv_b.reshape(depth, 1, CONV_WIDTH)
    lb3 = lb_all.reshape(depth, 1, HG_WIDTH)
    gn3 = hg_norm_g.reshape(depth, 1, HG_DV)
    lmg3, lmb3, lfg3, lfb3 = (a.reshape(depth, 1, d) for a in (ln_m_g, ln_m_b, ln_f_g, ln_f_b))
    for l in range(depth):
        mod_p = mod[l, :bp].reshape(bp, 1, 6 * d)
        mod_s = mod[l, bp:bp + db]

        (q, qf, k_all, v_all, kb, vt, km, yc, conv_new, hq, hlf, hv, hg, gates, kmt) = _proj_call(
            xp, mod_p, w_in, w_vt, conv_w, conv_b3, lb3, k_all, v_all, page_table, cache_kt, tm_proj, l)
        km3 = km.reshape(bp, nb, ATT_WIDTH)
        if nbp != nb:
            km3 = jnp.pad(km3, ((0, 0), (0, nbp - nb), (0, 0)))
        ya = _moba_call(q, qf, kb.reshape(bp, nb, MOBA_BLOCK, ATT_WIDTH), vt, km3)
        yh, hg_new = _hgrn_call(hq, hlf, hv, hg, gn3, th, l)
        xp = _mixffn_call(xp, ya, yc, yh, gates, mod_p, wa, wc, wh, wo, lmg3, lmb3,
                          wfi, wfo, lfg3, lfb3, tm_ffn, alpha, l)
        convs.append(conv_new)
        hgs.append(hg_new)

        proj = _sproj_call(xs, mod_s, w_in, 1024, l)
        q3 = proj[:, :ATT_WIDTH].reshape(db, 1, ATT_WIDTH)
        kn = proj[:, ATT_WIDTH:2 * ATT_WIDTH]
        vn = proj[:, 2 * ATT_WIDTH:O_CONV]
        sel = _ssel_call(q3, kmt, past_blocks)
        sel_flat = sel[:, :, :MOBA_TOPK].reshape(-1)
        ya_s = _sattn_call(sel_flat, page_table, q3, kn.reshape(db, 1, ATT_WIDTH),
                           vn.reshape(db, 1, ATT_WIDTH), cache_kt, cache_vt, l)
        yc_s, yh_s, gates_s, conv_s, hg_s = _sbranch_call(
            proj, conv_state2, state_hgrn, conv_w, conv_b3, lb3, gn3, min(8, db), l)
        xs3 = xs.reshape(1, db, d)
        mod_s3 = mod_s.reshape(1, db, 6 * d)
        xs3 = _mixffn_call(xs3, ya_s.reshape(1, db, ATT_WIDTH), yc_s.reshape(1, db, CONV_WIDTH),
                           yh_s.reshape(1, db, HG_WIDTH), gates_s.reshape(1, db, 3 * d), mod_s3,
                           wa, wc, wh, wo, lmg3, lmb3, wfi, wfo, lfg3, lfb3, db, alpha, l)
        xs = xs3.reshape(db, d)
        ks_s.append(kn.reshape(db, 1, ATT_HEADS, ATT_HEAD_DIM))
        vs_s.append(vn.reshape(db, 1, ATT_HEADS, ATT_HEAD_DIM))
        convs_s.append(conv_s.reshape(db, 2, CONV_WIDTH))
        hgs_s.append(hg_s)

    kv_shape = (depth, bp, seq, ATT_HEADS, ATT_HEAD_DIM)
    return (xp, xs.reshape(db, 1, d), k_all.reshape(kv_shape), v_all.reshape(kv_shape),
            jnp.stack(convs), jnp.stack(hgs),
            jnp.stack(ks_s), jnp.stack(vs_s), jnp.stack(convs_s), jnp.stack(hgs_s))
```

```python
import functools
import math

import jax
import jax.numpy as jnp
from jax import lax
from jax.experimental import pallas as pl
from jax.experimental.pallas import tpu as pltpu

F32 = jnp.float32
BF16 = jnp.bfloat16

ATT_HEADS = 8
ATT_HEAD_DIM = 64
ATT_WIDTH = ATT_HEADS * ATT_HEAD_DIM
MOBA_BLOCK = 256
MOBA_TOPK = 3
CONV_WIDTH = 512
HG_HEADS = 4
HG_DK = 128
HG_DV = 128
HG_WIDTH = HG_HEADS * HG_DK
PAGE_SIZE = 128
LN_EPS = 1e-5
NEG_INF = -1e30
ATT_SCALE = ATT_HEAD_DIM ** -0.5
HG_SCALE = HG_DK ** -0.5
LOG2E = math.log2(math.e)

LANES = 128
HEAD_PAIR = LANES // ATT_HEAD_DIM
HG_CHUNK = 32
VMEM_LIMIT = 56 * 1024 * 1024


def _cparams(*sem):
    return pltpu.CompilerParams(dimension_semantics=sem, vmem_limit_bytes=VMEM_LIMIT)


def _mm(a, b):
    return jnp.dot(a, b, preferred_element_type=F32)


def _mm_nt(a, b):
    return lax.dot_general(a, b, (((1,), (1,)), ((), ())), preferred_element_type=F32)


def _mm_tn(a, b):
    return lax.dot_general(a, b, (((0,), (0,)), ((), ())), preferred_element_type=F32)


def _split2(x):
    hi = x.astype(BF16)
    lo = (x - hi.astype(F32)).astype(BF16)
    return hi, lo


def _mm_nt_precise(a, b):
    ah, al = _split2(a)
    bh, bl = _split2(b)
    return _mm_nt(ah, bh) + (_mm_nt(ah, bl) + _mm_nt(al, bh))


def _mm_precise(a, b):
    ah, al = _split2(a)
    bh, bl = _split2(b)
    return _mm(ah, bh) + (_mm(ah, bl) + _mm(al, bh))


def _sigmoid(x):
    return 1.0 / (1.0 + jnp.exp(-x))


def _silu(x):
    return x * _sigmoid(x)


def _layer_norm(x, g, b):
    mu = jnp.mean(x, axis=-1, keepdims=True)
    xc = x - mu
    var = jnp.mean(xc * xc, axis=-1, keepdims=True)
    return xc * lax.rsqrt(var + LN_EPS) * g + b


def _modulate(x, sc_ref, sh_ref):
    return (x * (1.0 + sc_ref[...]) + sh_ref[...]).astype(BF16)


def _cumsum_rows(x):
    n = x.shape[0]
    row = lax.broadcasted_iota(jnp.int32, x.shape, 0)
    d = 1
    while d < n:
        x = x + jnp.where(row >= d, pltpu.roll(x, d, 0), 0.0)
        d *= 2
    return x


def _const_spec(shape):
    nd = len(shape)
    return pl.BlockSpec(shape, lambda *_: (0,) * nd)


def _layer_spec(stacked, layer, single_buffer=False):
    nd = stacked.ndim - 1
    kw = {"pipeline_mode": pl.Buffered(1)} if single_buffer else {}
    return pl.BlockSpec((None,) + tuple(stacked.shape[1:]), lambda *_: (layer,) + (0,) * nd, **kw)


def _ln_kernel(x_ref, g_ref, b_ref, o_ref):
    o_ref[...] = _layer_norm(x_ref[...], g_ref[...], b_ref[...])


def _ln_call(x2, g, b, tm):
    n, d = x2.shape
    return pl.pallas_call(
        _ln_kernel,
        grid=(n // tm,),
        in_specs=[pl.BlockSpec((tm, d), lambda i: (i, 0)), _const_spec((1, d)), _const_spec((1, d))],
        out_specs=pl.BlockSpec((tm, d), lambda i: (i, 0)),
        out_shape=jax.ShapeDtypeStruct((n, d), F32),
        compiler_params=_cparams("parallel"),
        name="ln_in",
    )(x2, g.reshape(1, d), b.reshape(1, d))


def _ada_kernel(c_ref, w_ref, b_ref, o_ref):
    a = _silu(c_ref[...]).astype(BF16)
    o_ref[...] = _mm(a, w_ref[...].astype(BF16)) + b_ref[...]


def _ada_call(c_all, w_ada, b_ada):
    depth, d, n6 = w_ada.shape
    mp = c_all.shape[0]
    tn = n6 // 4
    return pl.pallas_call(
        _ada_kernel,
        grid=(depth, n6 // tn),
        in_specs=[pl.BlockSpec((mp, d), lambda l, j: (0, 0)),
                  pl.BlockSpec((None, d, tn), lambda l, j: (l, 0, j)),
                  pl.BlockSpec((None, 1, tn), lambda l, j: (l, 0, j))],
        out_specs=pl.BlockSpec((None, mp, tn), lambda l, j: (l, 0, j)),
        out_shape=jax.ShapeDtypeStruct((depth, mp, n6), F32),
        compiler_params=_cparams("parallel", "parallel"),
        name="ada_mod",
    )(c_all, w_ada, b_ada.reshape(depth, 1, n6))


def _mod_spec(rows, d, chunk):
    return pl.BlockSpec((None, rows, d), lambda b, t, *_: (b, 0, chunk))


O_CONV = 3 * ATT_WIDTH
O_HG = O_CONV + 3 * CONV_WIDTH
O_GATE = O_HG + 4 * HG_WIDTH


def _cache_page_copies(step, slot, layer, pt_ref, kc_ref, pbuf, sem):
    pps = pbuf.shape[1]
    steps_per_seq = pt_ref.shape[1] // pps
    seq = step // steps_per_seq
    first = (step % steps_per_seq) * pps
    return [pltpu.make_async_copy(kc_ref.at[layer, pt_ref[seq, first + n]], pbuf.at[slot, n], sem.at[slot])
            for n in range(pps)]


def _cache_key_means(part, slot, pbuf, kmt_ref):
    pps = pbuf.shape[1]
    ppb = MOBA_BLOCK // PAGE_SIZE
    bps = pps // ppb
    lane = lax.broadcasted_iota(jnp.int32, (KMEAN_ROWS, LANES), 1)
    for rc in range(ATT_WIDTH // KMEAN_ROWS):
        rows = slice(rc * KMEAN_ROWS, (rc + 1) * KMEAN_ROWS)
        cols = jnp.zeros((KMEAN_ROWS, LANES), F32)
        for i in range(bps):
            tot = pbuf[slot, ppb * i, rows, :]
            for r in range(1, ppb):
                tot = tot + pbuf[slot, ppb * i + r, rows, :]
            mean = jnp.sum(tot, axis=-1, keepdims=True) * (1.0 / MOBA_BLOCK)
            cols = jnp.where(lane == part * bps + i, mean, cols)
        kmt_ref[rows, :] = kmt_ref[rows, :] + cols


def _proj_kernel(pt_ref, x_ref, sc_ref, sh_ref, w_ref, wvt_ref, cw_ref, cb_ref, lb_ref, kc_ref,
                 q_ref, qf_ref, k_ref, v_ref, kb_ref, vt_ref, km_ref,
                 yc_ref, cst_ref, hq_ref, lf_ref, hv_ref, hg_ref, gates_ref, kmt_ref,
                 carry_ref, pbuf, sem, *, layer):
    tm = x_ref.shape[0]
    t = pl.program_id(1)
    step = pl.program_id(0) * pl.num_programs(1) + t
    slot = step % 2
    dma_args = (layer, pt_ref, kc_ref, pbuf, sem)

    @pl.when(step == 0)
    def _():
        for cp in _cache_page_copies(step, slot, *dma_args):
            cp.start()

    @pl.when(step + 1 < pl.num_programs(0) * pl.num_programs(1))
    def _():
        for cp in _cache_page_copies(step + 1, 1 - slot, *dma_args):
            cp.start()

    @pl.when(t == 0)
    def _():
        carry_ref[...] = jnp.zeros_like(carry_ref)

    steps_per_seq = pt_ref.shape[1] // pbuf.shape[1]
    part = step % steps_per_seq

    @pl.when(part == 0)
    def _():
        kmt_ref[...] = jnp.zeros_like(kmt_ref)

    for cp in _cache_page_copies(step, slot, *dma_args):
        cp.wait()
    _cache_key_means(part, slot, pbuf, kmt_ref)

    u = _modulate(x_ref[...], sc_ref, sh_ref)

    p = _mm(u, w_ref[:, :O_CONV])
    q = p[:, :ATT_WIDTH]
    k = p[:, ATT_WIDTH:2 * ATT_WIDTH]
    q_ref[...] = (q * (ATT_SCALE * LOG2E)).astype(BF16)
    qf_ref[...] = q
    k_ref[...] = k
    v_ref[...] = p[:, 2 * ATT_WIDTH:]
    kb_ref[...] = k.astype(BF16)
    vt = _mm_nt(wvt_ref[...], u).astype(BF16)
    for i in range(tm // MOBA_BLOCK):
        rows = slice(i * MOBA_BLOCK, (i + 1) * MOBA_BLOCK)
        vt_ref[i] = vt[:, rows]
        km_ref[i] = jnp.sum(k[rows], axis=0, keepdims=True) * (1.0 / MOBA_BLOCK)

    p = _mm(u, w_ref[:, O_CONV:O_HG])
    ucv = p[:, CONV_WIDTH:2 * CONV_WIDTH] * p[:, 2 * CONV_WIDTH:]
    prev1 = carry_ref[7:8, :]
    prev2 = carry_ref[6:7, :]
    row = lax.broadcasted_iota(jnp.int32, ucv.shape, 0)
    s1 = jnp.where(row == 0, prev1, pltpu.roll(ucv, 1, 0))
    s2 = jnp.where(row == 0, prev2, jnp.where(row == 1, prev1, pltpu.roll(ucv, 2, 0)))
    cw = cw_ref[...]
    z = cb_ref[...] + s2 * cw[0:1, :] + s1 * cw[1:2, :] + ucv * cw[2:3, :]
    yc_ref[...] = (p[:, :CONV_WIDTH] * z).astype(BF16)
    carry_ref[...] = ucv[tm - 8:, :]
    cst_ref[...] = ucv[tm - 2:, :]

    p = _mm(u, w_ref[:, O_HG:O_GATE])
    lb = lb_ref[...]
    hq_ref[...] = p[:, :HG_WIDTH] * HG_SCALE
    lf_ref[...] = jnp.log(lb + (1.0 - lb) * _sigmoid(p[:, HG_WIDTH:2 * HG_WIDTH]))
    hv_ref[...] = p[:, 2 * HG_WIDTH:3 * HG_WIDTH].astype(BF16)
    hg_ref[...] = p[:, 3 * HG_WIDTH:]

    gates_ref[...] = _sigmoid(_mm(u, w_ref[:, O_GATE:])).astype(BF16)


def _proj_call(x3, mod3, w_in, w_vt, conv_w, conv_b, lb, k_all, v_all, page_table, cache_kt, tm, layer):
    b, s, d = x3.shape
    n_in = 10

    def body(*refs):
        _proj_kernel(*refs[:n_in], *refs[n_in + 2:], layer=layer)
    db, n_pages = page_table.shape
    n_steps = b * (s // tm)
    assert n_steps % db == 0 and n_pages % (n_steps // db) == 0
    steps_per_seq = n_steps // db
    pps = n_pages // steps_per_seq
    assert pps % (MOBA_BLOCK // PAGE_SIZE) == 0 and n_pages * PAGE_SIZE // MOBA_BLOCK <= LANES
    nt = s // tm
    kv_spec = pl.BlockSpec((None, None, tm, ATT_WIDTH), lambda bb, t, pt: (layer, bb, t, 0))
    kv_shape = jax.ShapeDtypeStruct((w_in.shape[0], b, s, ATT_WIDTH), F32)
    hbm = pl.BlockSpec(memory_space=pl.ANY)
    n_gate = w_in.shape[2] - O_GATE
    nbt = tm // MOBA_BLOCK
    nb = s // MOBA_BLOCK
    row = lambda bb, t, pt: (bb, t, 0)
    blk4 = lambda bb, t, pt: (bb, t, 0, 0)

    def rows(width):
        return pl.BlockSpec((None, tm, width), row)

    def arr(width, dtype):
        return jax.ShapeDtypeStruct((b, s, width), dtype)

    return pl.pallas_call(
        body,
        grid_spec=pltpu.PrefetchScalarGridSpec(
            num_scalar_prefetch=1,
            grid=(b, nt),
            in_specs=[rows(d), _mod_spec(1, d, 1), _mod_spec(1, d, 0),
                      _layer_spec(w_in, layer, True), _layer_spec(w_vt, layer, True),
                      _layer_spec(conv_w, layer), _layer_spec(conv_b, layer), _layer_spec(lb, layer),
                      hbm, hbm, hbm],
            out_specs=[rows(ATT_WIDTH), rows(ATT_WIDTH), kv_spec, kv_spec, rows(ATT_WIDTH)]
                      + [pl.BlockSpec((None, nbt, ATT_WIDTH, MOBA_BLOCK), blk4),
                         pl.BlockSpec((None, nbt, 1, ATT_WIDTH), blk4),
                         rows(CONV_WIDTH), pl.BlockSpec((None, 2, CONV_WIDTH), lambda bb, t, pt: (bb, 0, 0))]
                      + [rows(HG_WIDTH)] * 4 + [rows(n_gate)]
                      + [pl.BlockSpec((None, ATT_WIDTH, LANES),
                                      lambda bb, t, pt: ((bb * nt + t) // steps_per_seq, 0, 0))],
            scratch_shapes=[pltpu.VMEM((8, CONV_WIDTH), F32),
                            pltpu.VMEM((2, pps, ATT_WIDTH, PAGE_SIZE), F32),
                            pltpu.SemaphoreType.DMA((2,))],
        ),
        out_shape=[arr(ATT_WIDTH, BF16), arr(ATT_WIDTH, F32), kv_shape, kv_shape,
                   arr(ATT_WIDTH, BF16),
                   jax.ShapeDtypeStruct((b, nb, ATT_WIDTH, MOBA_BLOCK), BF16),
                   jax.ShapeDtypeStruct((b, nb, 1, ATT_WIDTH), F32),
                   arr(CONV_WIDTH, BF16), jax.ShapeDtypeStruct((b, 2, CONV_WIDTH), F32),
                   arr(HG_WIDTH, F32), arr(HG_WIDTH, F32), arr(HG_WIDTH, BF16), arr(HG_WIDTH, F32),
                   arr(n_gate, BF16),
                   jax.ShapeDtypeStruct((db, ATT_WIDTH, LANES), F32)],
        input_output_aliases={n_in: 2, n_in + 1: 3},
        compiler_params=_cparams("arbitrary", "arbitrary"),
        name="proj_in",
    )(page_table, x3, mod3, mod3, w_in, w_vt, conv_w, conv_b, lb, cache_kt, k_all, v_all)


MOBA_KEY_CHUNK = 64


SUBLANES = 8


def _moba_scores(j, kb_ref, qm_ref, s_ref, bm_ref):
    for h in range(ATT_HEADS):
        hp = h // HEAD_PAIR
        s = _mm_nt(kb_ref[j, :, hp * LANES:(hp + 1) * LANES], qm_ref[h])
        s_ref[h] = s
        bm_ref[h] = jnp.max(s.reshape(MOBA_BLOCK // SUBLANES, SUBLANES, MOBA_BLOCK), axis=0)


def _moba_softmax(rows, bias, s_ref, bm_ref, p_ref, ml_ref):
    nchunk = MOBA_BLOCK // MOBA_KEY_CHUNK
    chunks = [slice(c * MOBA_KEY_CHUNK, (c + 1) * MOBA_KEY_CHUNK) for c in range(nchunk)]
    alphas = []
    for h in range(ATT_HEADS):
        m = ml_ref[h:h + 1, :]
        if bias is None:
            blk_max = jnp.max(bm_ref[h], axis=0, keepdims=True)
        else:
            part = None
            for c in chunks:
                s = s_ref[h, c, :] + bias[c]
                s_ref[h, c, :] = s
                cmax = jnp.max(s.reshape(MOBA_KEY_CHUNK // SUBLANES, SUBLANES, MOBA_BLOCK), axis=0)
                part = cmax if part is None else jnp.maximum(part, cmax)
            blk_max = jnp.max(part, axis=0, keepdims=True)
        m_new = jnp.maximum(m, blk_max + rows[h])
        shift = jnp.where(rows[h] < 0.0, -NEG_INF, m_new)
        for c in chunks:
            p_ref[h, c, :] = jnp.exp2(s_ref[h, c, :] - shift).astype(BF16)
        ml_ref[h:h + 1, :] = m_new
        alphas.append(jnp.exp2(m - m_new))
    return alphas


MOBA_ACC_ROWS = ATT_HEAD_DIM + 16


def _moba_values(j, alphas, vt_ref, p_ref, acc_ref):
    ones = jnp.ones((MOBA_ACC_ROWS - ATT_HEAD_DIM, MOBA_BLOCK), BF16)
    for h in range(ATT_HEADS):
        vt_h = jnp.concatenate([vt_ref[j, h * ATT_HEAD_DIM:(h + 1) * ATT_HEAD_DIM, :], ones], axis=0)
        acc_ref[h] = alphas[h] * acc_ref[h] + _mm(vt_h, p_ref[h])


def _moba_kernel(q_ref, qf_ref, kb_ref, vt_ref, km_ref, o_ref,
                 qm_ref, sel_ref, ml_ref, acc_ref, s_ref, bm_ref, p_ref):
    nbp = km_ref.shape[0]
    blk = MOBA_BLOCK
    qi = pl.program_id(1)
    lane = lax.broadcasted_iota(jnp.int32, (blk, LANES), 1)
    blk_id = lax.broadcasted_iota(jnp.int32, (nbp, blk), 0)
    past = blk_id < qi
    for h in range(ATT_HEADS):
        hp, hh = divmod(h, HEAD_PAIR)
        cols = slice(hp * LANES, (hp + 1) * LANES)
        in_head = (lane >= hh * ATT_HEAD_DIM) & (lane < (hh + 1) * ATT_HEAD_DIM)
        q = q_ref[:, cols]
        qm_ref[h] = jnp.where(in_head, q, jnp.zeros_like(q))
        gate = _mm_nt_precise(km_ref[:, cols], jnp.where(in_head, qf_ref[:, cols], 0.0))
        gate = jnp.where(past, gate, NEG_INF)
        picked = blk_id < 0
        for _ in range(MOBA_TOPK):
            best = jnp.max(gate, axis=0, keepdims=True)
            first = jnp.min(jnp.where(gate == best, blk_id, nbp), axis=0, keepdims=True)
            hit = blk_id == first
            picked = picked | hit
            gate = jnp.where(hit, -jnp.inf, gate)
        sel_ref[h] = jnp.where(past & picked, 0.0, NEG_INF)
    ml_ref[...] = jnp.full((ATT_HEADS, blk), NEG_INF, F32)
    acc_ref[...] = jnp.zeros_like(acc_ref)

    _moba_scores(0, kb_ref, qm_ref, s_ref, bm_ref)

    def body(j, carry):
        rows = [sel_ref[h, pl.ds(j, 1), :] for h in range(ATT_HEADS)]
        alphas = _moba_softmax(rows, None, s_ref, bm_ref, p_ref, ml_ref)
        _moba_scores(j + 1, kb_ref, qm_ref, s_ref, bm_ref)
        _moba_values(j, alphas, vt_ref, p_ref, acc_ref)
        return carry

    lax.fori_loop(0, qi, body, 0)

    key = lax.broadcasted_iota(jnp.int32, (blk, blk), 0)
    qry = lax.broadcasted_iota(jnp.int32, (blk, blk), 1)
    causal = jnp.where(key <= qry, 0.0, NEG_INF)
    zero_row = jnp.zeros((1, blk), F32)
    alphas = _moba_softmax([zero_row] * ATT_HEADS, causal, s_ref, bm_ref, p_ref, ml_ref)
    _moba_values(qi, alphas, vt_ref, p_ref, acc_ref)
    outs = [acc_ref[h, :ATT_HEAD_DIM, :] / acc_ref[h, ATT_HEAD_DIM:ATT_HEAD_DIM + 1, :]
            for h in range(ATT_HEADS)]
    o_ref[...] = jnp.concatenate(outs, axis=0).T.astype(BF16)


def _moba_call(q, qf, kb4, vt4, km3):
    b, s, _ = q.shape
    nb = s // MOBA_BLOCK
    nbp = km3.shape[1]
    qspec = pl.BlockSpec((None, MOBA_BLOCK, ATT_WIDTH), lambda bb, qi: (bb, qi, 0))
    return pl.pallas_call(
        _moba_kernel,
        grid=(b, nb),
        in_specs=[qspec, qspec,
                  pl.BlockSpec((None, nb, MOBA_BLOCK, ATT_WIDTH), lambda bb, qi: (bb, 0, 0, 0)),
                  pl.BlockSpec((None, nb, ATT_WIDTH, MOBA_BLOCK), lambda bb, qi: (bb, 0, 0, 0)),
                  pl.BlockSpec((None, nbp, ATT_WIDTH), lambda bb, qi: (bb, 0, 0))],
        out_specs=qspec,
        out_shape=jax.ShapeDtypeStruct((b, s, ATT_WIDTH), BF16),
        scratch_shapes=[pltpu.VMEM((ATT_HEADS, MOBA_BLOCK, LANES), BF16),
                        pltpu.VMEM((ATT_HEADS, nbp, MOBA_BLOCK), F32),
                        pltpu.VMEM((ATT_HEADS, MOBA_BLOCK), F32),
                        pltpu.VMEM((ATT_HEADS, MOBA_ACC_ROWS, MOBA_BLOCK), F32),
                        pltpu.VMEM((ATT_HEADS, MOBA_BLOCK, MOBA_BLOCK), F32),
                        pltpu.VMEM((ATT_HEADS, SUBLANES, MOBA_BLOCK), F32),
                        pltpu.VMEM((ATT_HEADS, MOBA_BLOCK, MOBA_BLOCK), BF16)],
        compiler_params=_cparams("parallel", "arbitrary"),
        name="moba_prompt",
    )(q, qf, kb4, vt4, km3)


def _hgrn_kernel(q_ref, lf_ref, v_ref, g_ref, gn_ref, y_ref, so_ref,
                 st_ref, qg_ref, oi_ref, ds_ref, dec_ref, a_ref):
    th = q_ref.shape[0]
    c = HG_CHUNK
    t = pl.program_id(1)

    @pl.when(t == 0)
    def _():
        st_ref[...] = jnp.zeros_like(st_ref)

    tril = (lax.broadcasted_iota(jnp.int32, (c, c), 0) >= lax.broadcasted_iota(jnp.int32, (c, c), 1))
    for ci in range(th // c):
        rows = slice(ci * c, (ci + 1) * c)
        for h in range(HG_HEADS):
            lanes = slice(h * HG_DK, (h + 1) * HG_DK)
            lf = lf_ref[rows, lanes]
            q = q_ref[rows, lanes]
            v = v_ref[rows, lanes]
            g = _cumsum_rows(lf)
            gmid = g[c // 2 - 1:c // 2, :]
            glast = g[c - 1:c, :]
            k = 1.0 - jnp.exp(lf)
            a = _mm_nt((q * jnp.exp(g - gmid)).astype(BF16), (k * jnp.exp(gmid - g)).astype(BF16))
            a_ref[ci, h] = jnp.where(tril, a, 0.0).astype(BF16)
            qg_ref[rows, lanes] = (q * jnp.exp(g)).astype(BF16)
            kd = (k * jnp.exp(glast - g)).astype(BF16)
            ds_ref[ci, h] = _mm_tn(v, kd)
            dec_ref[ci:ci + 1, lanes] = jnp.exp(glast)

    for ci in range(th // c):
        rows = slice(ci * c, (ci + 1) * c)
        for h in range(HG_HEADS):
            lanes = slice(h * HG_DK, (h + 1) * HG_DK)
            oi_ref[rows, lanes] = _mm(a_ref[ci, h], v_ref[rows, lanes])

    gn = gn_ref[...]
    for h in range(HG_HEADS):
        lanes = slice(h * HG_DK, (h + 1) * HG_DK)
        st = st_ref[h]
        for ci in range(th // c):
            rows = slice(ci * c, (ci + 1) * c)
            o = _mm_nt(qg_ref[rows, lanes], st.astype(BF16)) + oi_ref[rows, lanes]
            st = st * dec_ref[ci:ci + 1, lanes] + ds_ref[ci, h]
            rms = lax.rsqrt(jnp.mean(o * o, axis=-1, keepdims=True) + LN_EPS)
            y_ref[rows, lanes] = (o * rms * gn * _silu(g_ref[rows, lanes])).astype(BF16)
        st_ref[h] = st

    @pl.when(t == pl.num_programs(1) - 1)
    def _():
        for h in range(HG_HEADS):
            so_ref[h] = st_ref[h].T


def _hgrn_call(q, lf, v, g, gn, th, layer):
    b, s, _ = q.shape
    row = lambda bb, t: (bb, t, 0)
    act = pl.BlockSpec((None, th, HG_WIDTH), row)
    return pl.pallas_call(
        _hgrn_kernel,
        grid=(b, s // th),
        in_specs=[act, act, act, act, _layer_spec(gn, layer)],
        out_specs=[act, pl.BlockSpec((None, HG_HEADS, HG_DK, HG_DV), lambda bb, t: (bb, 0, 0, 0))],
        out_shape=[jax.ShapeDtypeStruct((b, s, HG_WIDTH), BF16),
                   jax.ShapeDtypeStruct((b, HG_HEADS, HG_DK, HG_DV), F32)],
        scratch_shapes=[pltpu.VMEM((HG_HEADS, HG_DV, HG_DK), F32),
                        pltpu.VMEM((th, HG_WIDTH), BF16),
                        pltpu.VMEM((th, HG_WIDTH), F32),
                        pltpu.VMEM((th // HG_CHUNK, HG_HEADS, HG_DV, HG_DK), F32),
                        pltpu.VMEM((max(th // HG_CHUNK, SUBLANES), HG_WIDTH), F32),
                        pltpu.VMEM((th // HG_CHUNK, HG_HEADS, HG_CHUNK, HG_CHUNK), BF16)],
        compiler_params=_cparams("parallel", "arbitrary"),
        name="hgrn_prompt",
    )(q, lf, v, g, gn)


def _mixffn_kernel(x_ref, ya_ref, yc_ref, yh_ref, gates_ref, gtm_ref, scf_ref, shf_ref, gtf_ref,
                   wa_ref, wc_ref, wh_ref, wo_ref, lmg_ref, lmb_ref,
                   wfi_ref, wfo_ref, lfg_ref, lfb_ref, o_ref, *, alpha):
    d = x_ref.shape[1]
    dff = wfo_ref.shape[0]
    gates = gates_ref[...]
    merged = (gates[:, :d].astype(F32) * _mm(ya_ref[...], wa_ref[...])
              + gates[:, d:2 * d].astype(F32) * _mm(yc_ref[...], wc_ref[...])
              + gates[:, 2 * d:].astype(F32) * _mm(yh_ref[...], wh_ref[...]))
    y = _mm(merged.astype(BF16), wo_ref[...])
    x = _layer_norm(alpha * x_ref[...] + gtm_ref[...] * y, lmg_ref[...], lmb_ref[...])
    h = _mm(_modulate(x, scf_ref, shf_ref), wfi_ref[...])
    act = (_silu(h[:, :dff]) * h[:, dff:]).astype(BF16)
    y = _mm(act, wfo_ref[...])
    o_ref[...] = _layer_norm(alpha * x + gtf_ref[...] * y, lfg_ref[...], lfb_ref[...])


def _mixffn_call(x3, ya, yc, yh, gates, mod3, wa, wc, wh, wo, lmg, lmb, wfi, wfo, lfg, lfb, tm, alpha, layer):
    b, s, d = x3.shape
    mod_rows = mod3.shape[1]
    mrows = 1 if mod_rows == 1 else tm
    row = lambda bb, t: (bb, t, 0)
    br = pl.BlockSpec((None, tm, ya.shape[2]), row)
    xs = pl.BlockSpec((None, tm, d), row)
    vec = _layer_spec(lmg, layer)

    def weight(w):
        return _layer_spec(w, layer, True)

    return pl.pallas_call(
        functools.partial(_mixffn_kernel, alpha=alpha),
        grid=(b, s // tm),
        in_specs=[xs, br, br, br, pl.BlockSpec((None, tm, 3 * d), row),
                  _mod_spec(mrows, d, 2), _mod_spec(mrows, d, 4), _mod_spec(mrows, d, 3), _mod_spec(mrows, d, 5),
                  weight(wa), weight(wc), weight(wh), weight(wo), vec, vec,
                  weight(wfi), weight(wfo), vec, vec],
        out_specs=xs,
        out_shape=jax.ShapeDtypeStruct((b, s, d), F32),
        compiler_params=_cparams("parallel", "parallel"),
        name="mix_ffn",
    )(x3, ya, yc, yh, gates, mod3, mod3, mod3, mod3, wa, wc, wh, wo, lmg, lmb, wfi, wfo, lfg, lfb)


KMEAN_ROWS = 64


def _sproj_kernel(x_ref, sc_ref, sh_ref, w_ref, o_ref):
    o_ref[...] = _mm(_modulate(x_ref[...], sc_ref, sh_ref), w_ref[...])


def _sproj_call(x2, mod2, w_in, tn, layer):
    m, d = x2.shape
    n = w_in.shape[2]
    return pl.pallas_call(
        _sproj_kernel,
        grid=(n // tn,),
        in_specs=[pl.BlockSpec((m, d), lambda j: (0, 0)),
                  pl.BlockSpec((m, d), lambda j: (0, 1)),
                  pl.BlockSpec((m, d), lambda j: (0, 0)),
                  pl.BlockSpec((None, d, tn), lambda j: (layer, 0, j))],
        out_specs=pl.BlockSpec((m, tn), lambda j: (0, j)),
        out_shape=jax.ShapeDtypeStruct((m, n), F32),
        compiler_params=_cparams("parallel"),
        name="sample_proj",
    )(x2, mod2, mod2, w_in)


def _ssel_kernel(q_ref, kmt_ref, o_ref, *, nbs):
    head = lax.broadcasted_iota(jnp.int32, (ATT_HEADS, ATT_WIDTH), 0)
    dim = lax.broadcasted_iota(jnp.int32, (ATT_HEADS, ATT_WIDTH), 1)
    lane = lax.broadcasted_iota(jnp.int32, (ATT_HEADS, LANES), 1)
    for i in range(q_ref.shape[0]):
        q_heads = jnp.where(dim // ATT_HEAD_DIM == head, q_ref[i], 0.0)
        gate = _mm_precise(q_heads, kmt_ref[i])
        gate = jnp.where(lane < nbs, gate, -jnp.inf)
        out = jnp.zeros((ATT_HEADS, LANES), jnp.int32)
        for r in range(MOBA_TOPK):
            best = jnp.max(gate, axis=-1, keepdims=True)
            idx = jnp.min(jnp.where(gate == best, lane, LANES), axis=-1, keepdims=True)
            out = jnp.where(lane == r, idx, out)
            gate = jnp.where(lane == idx, -jnp.inf, gate)
        o_ref[i] = out


def _ssel_call(q3, kmt, nbs):
    db = q3.shape[0]
    group = math.gcd(db, 8)
    return pl.pallas_call(
        functools.partial(_ssel_kernel, nbs=nbs),
        grid=(db // group,),
        in_specs=[pl.BlockSpec((group, 1, ATT_WIDTH), lambda b: (b, 0, 0)),
                  pl.BlockSpec((group, ATT_WIDTH, LANES), lambda b: (b, 0, 0))],
        out_specs=pl.BlockSpec((group, ATT_HEADS, LANES), lambda b: (b, 0, 0)),
        out_shape=jax.ShapeDtypeStruct((db, ATT_HEADS, LANES), jnp.int32),
        compiler_params=_cparams("parallel"),
        name="sample_select",
    )(q3, kmt)


SEL_PAGES = MOBA_TOPK * (MOBA_BLOCK // PAGE_SIZE)


def _sattn_copies(seq, slot, layer, sel_ref, pt_ref, kc_ref, vc_ref, kbuf, vbuf, sem):
    ppb = MOBA_BLOCK // PAGE_SIZE
    copies = []
    for h in range(ATT_HEADS):
        rows = pl.ds(h * ATT_HEAD_DIM, ATT_HEAD_DIM)
        for i in range(MOBA_TOPK):
            blk = sel_ref[(seq * ATT_HEADS + h) * MOBA_TOPK + i]
            for r in range(ppb):
                page = pt_ref[seq, blk * ppb + r]
                toks = pl.ds((i * ppb + r) * PAGE_SIZE, PAGE_SIZE)
                copies.append(pltpu.make_async_copy(kc_ref.at[layer, page, rows, :],
                                                    kbuf.at[slot, h, :, toks], sem.at[0, slot]))
                copies.append(pltpu.make_async_copy(vc_ref.at[layer, page, rows, :],
                                                    vbuf.at[slot, h, :, toks], sem.at[1, slot]))
    return copies


def _sattn_kernel(sel_ref, pt_ref, q_ref, kn_ref, vn_ref, kc_ref, vc_ref, o_ref, kbuf, vbuf, sem, *, layer):
    b = pl.program_id(0)
    slot = b % 2
    args = (layer, sel_ref, pt_ref, kc_ref, vc_ref, kbuf, vbuf, sem)

    @pl.when(b == 0)
    def _():
        for cp in _sattn_copies(b, slot, *args):
            cp.start()

    @pl.when(b + 1 < pl.num_programs(0))
    def _():
        for cp in _sattn_copies(b + 1, 1 - slot, *args):
            cp.start()

    for cp in _sattn_copies(b, slot, *args):
        cp.wait()

    q = q_ref[...] * ATT_SCALE
    kn = kn_ref[...]
    vn = vn_ref[...]
    heads = [slice(h * ATT_HEAD_DIM, (h + 1) * ATT_HEAD_DIM) for h in range(ATT_HEADS)]
    scores = []
    for h, dims in enumerate(heads):
        q8 = jnp.broadcast_to(q[:, dims], (SUBLANES, ATT_HEAD_DIM)).astype(BF16)
        scores.append(_mm(q8, kbuf[slot, h].astype(BF16))[0:1, :])
    probs = []
    for h, dims in enumerate(heads):
        s_self = jnp.sum(q[:, dims] * kn[:, dims], axis=-1, keepdims=True)
        m = jnp.maximum(s_self, jnp.max(scores[h], axis=-1, keepdims=True))
        p_self = jnp.exp(s_self - m)
        p = jnp.exp(scores[h] - m)
        probs.append((p, p_self, p_self + jnp.sum(p, axis=-1, keepdims=True)))
    outs = []
    for h, dims in enumerate(heads):
        p, p_self, l = probs[h]
        p8 = jnp.broadcast_to(p, (SUBLANES, p.shape[1])).astype(BF16)
        acc = p_self * vn[:, dims] + _mm_nt(p8, vbuf[slot, h].astype(BF16))[0:1, :]
        outs.append(acc / l)
    o_ref[...] = jnp.concatenate(outs, axis=1).astype(BF16)


def _sattn_call(sel_flat, page_table, q3, kn3, vn3, cache_kt, cache_vt, layer):
    db = q3.shape[0]
    vec = pl.BlockSpec((None, 1, ATT_WIDTH), lambda b, sel, pt: (b, 0, 0))
    hbm = pl.BlockSpec(memory_space=pl.ANY)
    buf = pltpu.VMEM((2, ATT_HEADS, ATT_HEAD_DIM, SEL_PAGES * PAGE_SIZE), F32)
    return pl.pallas_call(
        functools.partial(_sattn_kernel, layer=layer),
        grid_spec=pltpu.PrefetchScalarGridSpec(
            num_scalar_prefetch=2,
            grid=(db,),
            in_specs=[vec, vec, vec, hbm, hbm],
            out_specs=vec,
            scratch_shapes=[buf, buf, pltpu.SemaphoreType.DMA((2, 2))],
        ),
        out_shape=jax.ShapeDtypeStruct((db, 1, ATT_WIDTH), BF16),
        compiler_params=_cparams("arbitrary"),
        name="sample_attn",
    )(sel_flat, page_table, q3, kn3, vn3, cache_kt, cache_vt)


def _sbranch_kernel(p_ref, cs_ref, hs_ref, cw_ref, cb_ref, lb_ref, gn_ref,
                    yc_ref, yh_ref, gates_ref, cso_ref, hso_ref):
    nb = p_ref.shape[0]
    p = p_ref[...]
    o0 = 3 * ATT_WIDTH
    gate_b = p[:, o0:o0 + CONV_WIDTH]
    ucv = p[:, o0 + CONV_WIDTH:o0 + 2 * CONV_WIDTH] * p[:, o0 + 2 * CONV_WIDTH:o0 + 3 * CONV_WIDTH]
    cs = cs_ref[...]
    prev2 = cs[:, :CONV_WIDTH]
    prev1 = cs[:, CONV_WIDTH:]
    cw = cw_ref[...]
    z = cb_ref[...] + prev2 * cw[0:1, :] + prev1 * cw[1:2, :] + ucv * cw[2:3, :]
    yc_ref[...] = (gate_b * z).astype(BF16)
    cso_ref[...] = jnp.concatenate([prev1, ucv], axis=1)

    o1 = o0 + 3 * CONV_WIDTH
    lb = lb_ref[...]
    hq = p[:, o1:o1 + HG_WIDTH] * HG_SCALE
    f = lb + (1.0 - lb) * _sigmoid(p[:, o1 + HG_WIDTH:o1 + 2 * HG_WIDTH])
    hv = p[:, o1 + 2 * HG_WIDTH:o1 + 3 * HG_WIDTH]
    hg = p[:, o1 + 3 * HG_WIDTH:o1 + 4 * HG_WIDTH]
    qt = hq.T
    ft = f.T
    kt = (1.0 - f).T
    gn = gn_ref[...]
    rows = []
    for b in range(nb):
        heads = []
        for h in range(HG_HEADS):
            ks = slice(h * HG_DK, (h + 1) * HG_DK)
            s_new = ft[ks, b:b + 1] * hs_ref[b, h] + kt[ks, b:b + 1] * hv[b:b + 1, ks]
            hso_ref[b, h] = s_new
            o = jnp.sum(qt[ks, b:b + 1] * s_new, axis=0, keepdims=True)
            rms = lax.rsqrt(jnp.mean(o * o, axis=-1, keepdims=True) + LN_EPS)
            heads.append(o * rms * gn)
        rows.append(jnp.concatenate(heads, axis=1))
    yh_ref[...] = (jnp.concatenate(rows, axis=0) * _silu(hg)).astype(BF16)
    gates_ref[...] = _sigmoid(p[:, o1 + 4 * HG_WIDTH:]).astype(BF16)


def _sbranch_call(proj, conv_state2, hg_state, conv_w, conv_b, lb, gn, nb, layer):
    db, n_in = proj.shape
    d3 = n_in - 3 * ATT_WIDTH - 3 * CONV_WIDTH - 4 * HG_WIDTH
    row = lambda g: (g, 0)
    st = pl.BlockSpec((nb, HG_HEADS, HG_DK, HG_DV), lambda g: (g, 0, 0, 0))
    return pl.pallas_call(
        _sbranch_kernel,
        grid=(db // nb,),
        in_specs=[pl.BlockSpec((nb, n_in), row),
                  pl.BlockSpec((None, nb, 2 * CONV_WIDTH), lambda g: (layer, g, 0)),
                  pl.BlockSpec((None, nb, HG_HEADS, HG_DK, HG_DV), lambda g: (layer, g, 0, 0, 0)),
                  _layer_spec(conv_w, layer), _layer_spec(conv_b, layer), _layer_spec(lb, layer),
                  _layer_spec(gn, layer)],
        out_specs=[pl.BlockSpec((nb, CONV_WIDTH), row), pl.BlockSpec((nb, HG_WIDTH), row),
                   pl.BlockSpec((nb, d3), row), pl.BlockSpec((nb, 2 * CONV_WIDTH), row), st],
        out_shape=[jax.ShapeDtypeStruct((db, CONV_WIDTH), BF16),
                   jax.ShapeDtypeStruct((db, HG_WIDTH), BF16),
                   jax.ShapeDtypeStruct((db, d3), BF16),
                   jax.ShapeDtypeStruct((db, 2 * CONV_WIDTH), F32),
                   jax.ShapeDtypeStruct(hg_state.shape[1:], F32)],
        compiler_params=_cparams("parallel"),
        name="sample_branch",
    )(proj, conv_state2, hg_state, conv_w, conv_b, lb, gn)


def kernel(x_prompt, x_sample, cache_k, cache_v, state_conv, state_hgrn, page_table, c_prompt, c_sample,
           ln_in_g, ln_in_b, w_ada, b_ada, w_mix_in, conv_w, conv_b, hg_lb_logits, hg_norm_g,
           w_br_attn, w_br_conv, w_br_hgrn, w_mix_out, ln_m_g, ln_m_b, w_ffn_in, w_ffn_out, ln_f_g, ln_f_b):
    bp, seq, d = x_prompt.shape
    db = x_sample.shape[0]
    depth = w_ada.shape[0]
    n_pages = page_table.shape[1]
    past_blocks = n_pages * PAGE_SIZE // MOBA_BLOCK
    assert x_sample.shape[1] == 1 and seq % MOBA_BLOCK == 0
    assert past_blocks >= MOBA_TOPK and (n_pages * PAGE_SIZE) % MOBA_BLOCK == 0
    alpha = (2 * depth) ** 0.25
    nb = seq // MOBA_BLOCK
    nbp = -(-nb // 8) * 8

    tm = min(512, seq)
    tm_proj = min(256, seq)
    tm_ffn = min(512, seq)
    th = min(256, seq)

    w_in = w_mix_in.astype(BF16)
    w_vt = jnp.swapaxes(w_in[:, :, 2 * ATT_WIDTH:O_CONV], 1, 2)
    wa, wc, wh, wo = (w.astype(BF16) for w in (w_br_attn, w_br_conv, w_br_hgrn, w_mix_out))
    wfi, wfo = w_ffn_in.astype(BF16), w_ffn_out.astype(BF16)
    lb_cum = jnp.cumsum(jax.nn.softmax(hg_lb_logits.astype(F32), axis=0), axis=0)
    lb_all = lb_cum - lb_cum[0]

    n_c = bp + db
    mp = -(-n_c // 8) * 8
    c_all = jnp.concatenate([c_prompt, c_sample, jnp.zeros((mp - n_c, d), F32)], axis=0)
    mod = _ada_call(c_all, w_ada, b_ada)

    cache_kt = jnp.transpose(cache_k, (0, 1, 3, 4, 2)).reshape(depth, cache_k.shape[1], ATT_WIDTH, PAGE_SIZE)
    cache_vt = jnp.transpose(cache_v, (0, 1, 3, 4, 2)).reshape(depth, cache_v.shape[1], ATT_WIDTH, PAGE_SIZE)

    xp = _ln_call(x_prompt.reshape(bp * seq, d), ln_in_g, ln_in_b, tm).reshape(bp, seq, d)
    xs = _ln_call(x_sample.reshape(db, d), ln_in_g, ln_in_b, db)

    convs, hgs = [], []
    ks_s, vs_s, convs_s, hgs_s = [], [], [], []
    conv_state2 = state_conv.reshape(depth, db, 2 * CONV_WIDTH)
    k_all = jnp.zeros((depth, bp, seq, ATT_WIDTH), F32)
    v_all = jnp.zeros((depth, bp, seq, ATT_WIDTH), F32)
    conv_b3 = conv_b.reshape(depth, 1, CONV_WIDTH)
    lb3 = lb_all.reshape(depth, 1, HG_WIDTH)
    gn3 = hg_norm_g.reshape(depth, 1, HG_DV)
    lmg3, lmb3, lfg3, lfb3 = (a.reshape(depth, 1, d) for a in (ln_m_g, ln_m_b, ln_f_g, ln_f_b))
    for l in range(depth):
        mod_p = mod[l, :bp].reshape(bp, 1, 6 * d)
        mod_s = mod[l, bp:bp + db]

        (q, qf, k_all, v_all, kb, vt, km, yc, conv_new, hq, hlf, hv, hg, gates, kmt) = _proj_call(
            xp, mod_p, w_in, w_vt, conv_w, conv_b3, lb3, k_all, v_all, page_table, cache_kt, tm_proj, l)
        km3 = km.reshape(bp, nb, ATT_WIDTH)
        if nbp != nb:
            km3 = jnp.pad(km3, ((0, 0), (0, nbp - nb), (0, 0)))
        ya = _moba_call(q, qf, kb.reshape(bp, nb, MOBA_BLOCK, ATT_WIDTH), vt, km3)
        yh, hg_new = _hgrn_call(hq, hlf, hv, hg, gn3, th, l)
        xp = _mixffn_call(xp, ya, yc, yh, gates, mod_p, wa, wc, wh, wo, lmg3, lmb3,
                          wfi, wfo, lfg3, lfb3, tm_ffn, alpha, l)
        convs.append(conv_new)
        hgs.append(hg_new)

        proj = _sproj_call(xs, mod_s, w_in, 1024, l)
        q3 = proj[:, :ATT_WIDTH].reshape(db, 1, ATT_WIDTH)
        kn = proj[:, ATT_WIDTH:2 * ATT_WIDTH]
        vn = proj[:, 2 * ATT_WIDTH:O_CONV]
        sel = _ssel_call(q3, kmt, past_blocks)
        sel_flat = sel[:, :, :MOBA_TOPK].reshape(-1)
        ya_s = _sattn_call(sel_flat, page_table, q3, kn.reshape(db, 1, ATT_WIDTH),
                           vn.reshape(db, 1, ATT_WIDTH), cache_kt, cache_vt, l)
        yc_s, yh_s, gates_s, conv_s, hg_s = _sbranch_call(
            proj, conv_state2, state_hgrn, conv_w, conv_b3, lb3, gn3, min(8, db), l)
        xs3 = xs.reshape(1, db, d)
        mod_s3 = mod_s.reshape(1, db, 6 * d)
        xs3 = _mixffn_call(xs3, ya_s.reshape(1, db, ATT_WIDTH), yc_s.reshape(1, db, CONV_WIDTH),
                           yh_s.reshape(1, db, HG_WIDTH), gates_s.reshape(1, db, 3 * d), mod_s3,
                           wa, wc, wh, wo, lmg3, lmb3, wfi, wfo, lfg3, lfb3, db, alpha, l)
        xs = xs3.reshape(db, d)
        ks_s.append(kn.reshape(db, 1, ATT_HEADS, ATT_HEAD_DIM))
        vs_s.append(vn.reshape(db, 1, ATT_HEADS, ATT_HEAD_DIM))
        convs_s.append(conv_s.reshape(db, 2, CONV_WIDTH))
        hgs_s.append(hg_s)

    kv_shape = (depth, bp, seq, ATT_HEADS, ATT_HEAD_DIM)
    return (xp, xs.reshape(db, 1, d), k_all.reshape(kv_shape), v_all.reshape(kv_shape),
            jnp.stack(convs), jnp.stack(hgs),
            jnp.stack(ks_s), jnp.stack(vs_s), jnp.stack(convs_s), jnp.stack(hgs_s))
```
